```python
import math
import jax, jax.numpy as jnp
from jax import lax
import numpy as np

D_MODEL = 4096
BATCH = 1
SEQ = 8192
DEPTH = 2
DEC_BATCH = 16
DEC_SEQ = 32
PAST_LEN = 4096

CHUNK = 64
D_MIX = D_MODEL
GROUP_W = D_MIX // 4

SSD_HEADDIM = 64
SSD_HEADS = GROUP_W // SSD_HEADDIM
SSD_STATE = 128
SSD_GROUPS = 2
SSD_CONV = 4
SSD_CONV_CH = GROUP_W + 2 * SSD_GROUPS * SSD_STATE
SSD_IN = GROUP_W + SSD_CONV_CH + SSD_HEADS

ML_HEADS = 4
ML_DV = GROUP_W // ML_HEADS
ML_DK = ML_DV // 2
ML_IN = 2 * ML_HEADS * ML_DK + GROUP_W + 2 * ML_HEADS + GROUP_W

HG_EXPAND = 128
HG_HEADS = GROUP_W // HG_EXPAND
HG_DK = HG_EXPAND
HG_DV = GROUP_W // HG_HEADS
HG_IN = 4 * GROUP_W

RW_HEADDIM = 64
RW_HEADS = GROUP_W // RW_HEADDIM
RW_LR_W = 64
RW_LR_A = 64
RW_LR_G = 128
RW_IN = 3 * GROUP_W + RW_LR_W + RW_LR_A + RW_LR_G
RW_LN_EPS = 64e-5

IN_PROJ = SSD_IN + ML_IN + HG_IN + RW_IN

N_EXPERTS = 32
TOP_K = 4
D_FF = D_MODEL
SWIGLU_LIMIT = 7.0
SWIGLU_ALPHA = 1.702

DN_ALPHA = (2 * DEPTH) ** 0.25
DN_BETA = (8 * DEPTH) ** -0.25
LN_EPS = 1e-5
RMS_EPS = 1e-6
NEG_BIG = -1e30
N_STATES = 8

kernel_name = 'hybrid_ssd_mlstm_hgrn2_rwkv7_moe_stream_step'


def _split(u, sizes):
    idx = [int(i) for i in np.cumsum(sizes)[:-1]]
    return jnp.split(u, idx, axis=-1)


def _layer_norm(x, g, b, eps=LN_EPS):
    xf = x.astype(jnp.float32)
    mu = jnp.mean(xf, axis=-1, keepdims=True)
    var = jnp.mean(jnp.square(xf - mu), axis=-1, keepdims=True)
    return ((xf - mu) * lax.rsqrt(var + eps) * g + b).astype(x.dtype)


def _rms_norm(x, g):
    xf = x.astype(jnp.float32)
    return (xf * lax.rsqrt(jnp.mean(xf * xf, axis=-1, keepdims=True) + RMS_EPS) * g).astype(x.dtype)


def _chunks(a, L):
    b_, t_ = a.shape[:2]
    return jnp.moveaxis(a.reshape((b_, t_ // L, L) + a.shape[2:]), 1, 0)


def _unchunks(a):
    a = jnp.moveaxis(a, 0, 1)
    return a.reshape((a.shape[0], a.shape[1] * a.shape[2]) + a.shape[3:])


def _masked_exp(mask, diff):
    return jnp.where(mask, jnp.exp(jnp.where(mask, diff, 0.0)), 0.0)


def _causal_conv(x, buf, w, b):
    width = w.shape[0]
    t_ = x.shape[1]
    xp = jnp.concatenate([buf.astype(x.dtype), x], axis=1)
    y = b + sum(xp[:, j:j + t_] * w[j] for j in range(width))
    return y, xp[:, t_:]


def _ssd_scan(x, dt, la, bm, cm, s0):
    L = math.gcd(x.shape[1], CHUNK)
    mask = jnp.tril(jnp.ones((L, L), bool))[None, :, :, None, None]

    def step(s, inp):
        xc, dtc, lac, bc, cc = inp
        cum = jnp.cumsum(lac, axis=1)
        dec = _masked_exp(mask, cum[:, :, None] - cum[:, None])
        w = jnp.einsum('btgn,bsgn->btsg', cc, bc)[..., None] * dec * dtc[:, None]
        y = jnp.einsum('btsgr,bsgrp->btgrp', w, xc)
        y = y + jnp.einsum('btgn,bgrpn->btgrp', cc, s) * jnp.exp(cum)[..., None]
        w_end = jnp.exp(cum[:, -1:] - cum) * dtc
        s = s * jnp.exp(cum[:, -1])[..., None, None] + jnp.einsum('bsgr,bsgrp,bsgn->bgrpn', w_end, xc, bc)
        return s, y

    s, y = lax.scan(step, s0, tuple(_chunks(a, L) for a in (x, dt, la, bm, cm)))
    return _unchunks(y), s


def _mlstm_scan(q, k, v, ig, lf, c0, n0, m0):
    L = math.gcd(q.shape[1], CHUNK)
    mask = jnp.tril(jnp.ones((L, L), bool))[None, :, :, None]

    def step(carry, inp):
        c, n, m = carry
        qc, kc, vc, ic, fc = inp
        b = jnp.cumsum(fc, axis=1)
        d = jnp.where(mask, b[:, :, None] - b[:, None] + ic[:, None], NEG_BIG)
        g = b + m[:, None]
        mt = jnp.maximum(g, jnp.max(d, axis=2))
        wts = jnp.where(mask, jnp.exp(d - mt[:, :, None]), 0.0)
        inter = jnp.exp(g - mt)
        qk = jnp.einsum('bthd,bshd->btsh', qc, kc) * wts
        num = jnp.einsum('btsh,bshv->bthv', qk, vc) + inter[..., None] * jnp.einsum('bthd,bhvd->bthv', qc, c)
        den = jnp.sum(qk, axis=2) + inter * jnp.einsum('bthd,bhd->bth', qc, n)
        h = num / jnp.maximum(jnp.abs(den), jnp.exp(-mt))[..., None]
        bl = b[:, -1]
        loc = bl[:, None] - b + ic
        m_new = jnp.maximum(bl + m, jnp.max(loc, axis=1))
        wl = jnp.exp(loc - m_new[:, None])
        decay = jnp.exp(bl + m - m_new)
        c = decay[..., None, None] * c + jnp.einsum('bsh,bshv,bshd->bhvd', wl, vc, kc)
        n = decay[..., None] * n + jnp.einsum('bsh,bshd->bhd', wl, kc)
        return (c, n, m_new), h

    (c, n, m), h = lax.scan(step, (c0, n0, m0), tuple(_chunks(a, L) for a in (q, k, v, ig, lf)))
    return _unchunks(h), c, n, m


def _gla_scan(q, k, lg, v, s0):
    L = math.gcd(q.shape[1], CHUNK)
    mask = jnp.tril(jnp.ones((L, L), bool))[None, :, :, None, None]

    def step(s, inp):
        qc, kc, gc, vc = inp
        G = jnp.cumsum(gc, axis=1)
        dec = _masked_exp(mask, G[:, :, None] - G[:, None])
        a = jnp.einsum('btshk,bshk->btsh', qc[:, :, None] * dec, kc)
        o = jnp.einsum('btsh,bshv->bthv', a, vc) + jnp.einsum('bthk,bhkv->bthv', qc * jnp.exp(G), s)
        s = jnp.exp(G[:, -1])[..., None] * s + jnp.einsum('bshk,bshv->bhkv', kc * jnp.exp(G[:, -1:] - G), vc)
        return s, o

    s, o = lax.scan(step, s0, tuple(_chunks(a, L) for a in (q, k, lg, v)))
    return _unchunks(o), s


def _rwkv7_scan(r, w, k, v, kk, a, s0):
    def step(s, inp):
        r_t, w_t, k_t, v_t, kk_t, a_t = inp
        sa = jnp.einsum('bhvk,bhk->bhv', s, -kk_t)
        s = s * w_t[:, :, None, :] + sa[..., None] * (kk_t * a_t)[:, :, None, :] + v_t[..., None] * k_t[:, :, None, :]
        return s, jnp.einsum('bhvk,bhk->bhv', s, r_t)

    s, y = lax.scan(step, s0, tuple(jnp.moveaxis(z, 1, 0) for z in (r, w, k, v, kk, a)))
    return jnp.moveaxis(y, 0, 1), s


def _ssd_mixer(u, conv_buf, s0, conv_w, conv_b, dt_bias, a_log, d_skip, norm_w):
    b_, t_ = u.shape[:2]
    f32 = jnp.float32
    rep = SSD_HEADS // SSD_GROUPS
    z, xbc, dt = _split(u, [GROUP_W, SSD_CONV_CH, SSD_HEADS])
    xbc, conv_new = _causal_conv(xbc, conv_buf, conv_w, conv_b)
    xbc = jax.nn.silu(xbc.astype(f32))
    xh, bm, cm = _split(xbc, [GROUP_W, SSD_GROUPS * SSD_STATE, SSD_GROUPS * SSD_STATE])
    xh = xh.reshape(b_, t_, SSD_GROUPS, rep, SSD_HEADDIM)
    bm = bm.reshape(b_, t_, SSD_GROUPS, SSD_STATE)
    cm = cm.reshape(b_, t_, SSD_GROUPS, SSD_STATE)
    dt = jax.nn.softplus(dt.astype(f32) + dt_bias).reshape(b_, t_, SSD_GROUPS, rep)
    la = dt * (-jnp.exp(a_log.astype(f32))).reshape(SSD_GROUPS, rep)
    s0 = s0.astype(f32).reshape(b_, SSD_GROUPS, rep, SSD_HEADDIM, SSD_STATE)
    y, s = _ssd_scan(xh, dt, la, bm, cm, s0)
    y = y + d_skip.astype(f32).reshape(SSD_GROUPS, rep, 1) * xh
    gw = GROUP_W // SSD_GROUPS
    y = y.reshape(b_, t_, SSD_GROUPS, gw) * jax.nn.silu(z.astype(f32)).reshape(b_, t_, SSD_GROUPS, gw)
    y = _rms_norm(y, norm_w.reshape(SSD_GROUPS, gw))
    return y.reshape(b_, t_, GROUP_W).astype(u.dtype), conv_new, s.reshape(b_, SSD_HEADS, SSD_HEADDIM, SSD_STATE)


def _mlstm_mixer(u, c0, n0, m0, b_if, norm_w):
    b_, t_ = u.shape[:2]
    f32 = jnp.float32
    qk_w = ML_HEADS * ML_DK
    q, k, v, ig, fg, o = _split(u.astype(f32), [qk_w, qk_w, GROUP_W, ML_HEADS, ML_HEADS, GROUP_W])
    q = q.reshape(b_, t_, ML_HEADS, ML_DK) * ML_DK ** -0.5
    k = k.reshape(b_, t_, ML_HEADS, ML_DK)
    v = v.reshape(b_, t_, ML_HEADS, ML_DV)
    ig = ig + b_if[:ML_HEADS]
    lf = jax.nn.log_sigmoid(fg + b_if[ML_HEADS:])
    h, c, n, m = _mlstm_scan(q, k, v, ig, lf, c0.astype(f32), n0.astype(f32), m0.astype(f32))
    h = _rms_norm(h, norm_w.reshape(ML_HEADS, ML_DV)).reshape(b_, t_, GROUP_W)
    return (h * jax.nn.sigmoid(o)).astype(u.dtype), c, n, m


def _hgrn2_mixer(u, s0, lb, norm_w):
    b_, t_ = u.shape[:2]
    f32 = jnp.float32
    q, fp, i, g = _split(u.astype(f32), [GROUP_W] * 4)
    shp = (b_, t_, HG_HEADS, HG_DK)
    q = jax.nn.silu(q).reshape(shp)
    f = lb + (1.0 - lb) * jax.nn.sigmoid(fp)
    lg = jnp.log(f).reshape(shp)
    k = ((1.0 - lb) * jax.nn.sigmoid(-fp)).reshape(shp)
    v = i.reshape(b_, t_, HG_HEADS, HG_DV)
    o, s = _gla_scan(q, k, lg, v, s0.astype(f32))
    o = _rms_norm(o, norm_w.reshape(HG_HEADS, HG_DV)).reshape(b_, t_, GROUP_W)
    return (o * jax.nn.silu(g)).astype(u.dtype), s


def _rwkv7_mixer(u, shift0, s0, mu, w0, w2, a0, a2, g2, k_k, k_a, r_k, ln_w, ln_b):
    b_, t_ = u.shape[:2]
    odt = u.dtype
    f32 = jnp.float32
    uf = u.astype(f32)
    prev = jnp.concatenate([shift0.astype(f32)[:, None], uf[:, :-1]], axis=1)
    um = uf + (prev - uf) * mu
    r, k, v, xw, xa, xg = _split(um, [GROUP_W] * 3 + [RW_LR_W, RW_LR_A, RW_LR_G])
    w_log = -jax.nn.softplus(-(w0 + jnp.tanh(xw) @ w2)) - 0.5
    decay = jnp.exp(-jnp.exp(w_log))
    a = jax.nn.sigmoid(a0 + xa @ a2)
    g = jax.nn.sigmoid(xg) @ g2
    shp = (b_, t_, RW_HEADS, RW_HEADDIM)
    kk = (k * k_k).reshape(shp)
    kk = kk / jnp.maximum(jnp.sqrt(jnp.sum(kk * kk, axis=-1, keepdims=True)), 1e-12)
    k = (k * (1.0 + (a - 1.0) * k_a)).reshape(shp)
    r = r.reshape(shp)
    v = v.reshape(shp)
    y, s = _rwkv7_scan(r, decay.reshape(shp), k, v, kk, a.reshape(shp), s0.astype(f32))
    y = _layer_norm(y, ln_w.reshape(RW_HEADS, RW_HEADDIM), ln_b.reshape(RW_HEADS, RW_HEADDIM), RW_LN_EPS)
    y = y + jnp.sum(r * k * r_k, axis=-1, keepdims=True) * v
    return (y.reshape(b_, t_, GROUP_W) * g).astype(odt), uf[:, -1], s


def _token_mixers(u, st, p, lb):
    ssd_conv, ssd_s, ml_c, ml_n, ml_m, hg_s, rw_shift, rw_s = st
    (conv_w, conv_b, dt_bias, a_log, d_skip, ssd_nw, ml_bif, ml_nw, hg_nw,
     rw_mu, rw_w0, rw_w2, rw_a0, rw_a2, rw_g2, rw_kk, rw_ka, rw_rk, rw_lnw, rw_lnb) = p
    ua, ub, uc, ud = _split(u, [SSD_IN, ML_IN, HG_IN, RW_IN])
    ya, ssd_conv, ssd_s = _ssd_mixer(ua, ssd_conv, ssd_s, conv_w, conv_b, dt_bias, a_log, d_skip, ssd_nw)
    yb, ml_c, ml_n, ml_m = _mlstm_mixer(ub, ml_c, ml_n, ml_m, ml_bif, ml_nw)
    yc, hg_s = _hgrn2_mixer(uc, hg_s, lb, hg_nw)
    yd, rw_shift, rw_s = _rwkv7_mixer(ud, rw_shift, rw_s, rw_mu, rw_w0, rw_w2, rw_a0, rw_a2, rw_g2,
                                      rw_kk, rw_ka, rw_rk, rw_lnw, rw_lnb)
    y = jnp.concatenate([ya, yb, yc, yd], axis=-1)
    return y, (ssd_conv, ssd_s, ml_c, ml_n, ml_m, hg_s, rw_shift, rw_s)


def _moe(t, router_w, router_b, w_gu, b_gu, w_down, b_down):
    logits = (t @ router_w + router_b).astype(jnp.float32)
    top_v, top_i = lax.top_k(logits, TOP_K)
    gates = jnp.sum(jax.nn.softmax(top_v, axis=-1)[..., None]
                    * jax.nn.one_hot(top_i, N_EXPERTS, dtype=jnp.float32), axis=1).astype(t.dtype)
    out = jnp.zeros_like(t)
    for e in range(N_EXPERTS):
        hg, hu = jnp.split(t @ w_gu[e] + b_gu[e], 2, axis=-1)
        hg = jnp.minimum(hg, SWIGLU_LIMIT)
        hu = jnp.clip(hu, -SWIGLU_LIMIT, SWIGLU_LIMIT)
        act = hg * jax.nn.sigmoid(SWIGLU_ALPHA * hg) * (hu + 1.0)
        out = out + gates[:, e:e + 1] * (act @ w_down[e] + b_down[e])
    return out


def setup_inputs(seed: int = 0) -> dict:
    key = jax.random.key(seed)
    ks = iter(jax.random.split(key, 64))
    f32 = jnp.float32

    def nrm(shape, scale):
        return scale * jax.random.normal(next(ks), shape, f32)

    def unif(shape, lo, hi):
        return jax.random.uniform(next(ks), shape, f32, lo, hi)

    dt0 = jnp.exp(unif((DEPTH, SSD_HEADS), math.log(1e-3), math.log(1e-1)))
    return {
        'x_prompt': nrm((BATCH, SEQ, D_MODEL), 1.0),
        'x_sample': nrm((DEC_BATCH, DEC_SEQ, D_MODEL), 1.0),
        'state_ssd_conv': nrm((DEPTH, DEC_BATCH, SSD_CONV - 1, SSD_CONV_CH), 1.0),
        'state_ssd': nrm((DEPTH, DEC_BATCH, SSD_HEADS, SSD_HEADDIM, SSD_STATE), 0.5),
        'state_mlstm_c': nrm((DEPTH, DEC_BATCH, ML_HEADS, ML_DV, ML_DK), 0.1),
        'state_mlstm_n': nrm((DEPTH, DEC_BATCH, ML_HEADS, ML_DK), 0.1),
        'state_mlstm_m': nrm((DEPTH, DEC_BATCH, ML_HEADS), 1.0),
        'state_hgrn': nrm((DEPTH, DEC_BATCH, HG_HEADS, HG_DK, HG_DV), 0.5),
        'state_rwkv_shift': nrm((DEPTH, DEC_BATCH, RW_IN), 1.0),
        'state_rwkv': nrm((DEPTH, DEC_BATCH, RW_HEADS, RW_HEADDIM, RW_HEADDIM), 0.2),
        'w_in': nrm((DEPTH, D_MODEL, IN_PROJ), D_MODEL ** -0.5),
        'ssd_conv_w': nrm((DEPTH, SSD_CONV, SSD_CONV_CH), 0.5),
        'ssd_conv_b': nrm((DEPTH, SSD_CONV_CH), 0.02),
        'ssd_dt_bias': dt0 + jnp.log(-jnp.expm1(-dt0)),
        'ssd_a_log': jnp.log(unif((DEPTH, SSD_HEADS), 1.0, 16.0)),
        'ssd_d': 1.0 + nrm((DEPTH, SSD_HEADS), 0.1),
        'ssd_norm_w': 1.0 + nrm((DEPTH, GROUP_W), 0.02),
        'ml_b_if': jnp.concatenate([nrm((DEPTH, ML_HEADS), 0.1), unif((DEPTH, ML_HEADS), 3.0, 6.0)], axis=-1),
        'ml_norm_w': 1.0 + nrm((DEPTH, GROUP_W), 0.02),
        'hg_lower_bounds': nrm((DEPTH, GROUP_W), 0.1),
        'hg_norm_w': 1.0 + nrm((DEPTH, GROUP_W), 0.02),
        'rw_mu': unif((DEPTH, RW_IN), 0.0, 1.0),
        'rw_w0': unif((DEPTH, GROUP_W), -6.0, -1.0),
        'rw_w2': nrm((DEPTH, RW_LR_W, GROUP_W), 0.1),
        'rw_a0': nrm((DEPTH, GROUP_W), 0.1),
        'rw_a2': nrm((DEPTH, RW_LR_A, GROUP_W), 0.5 * RW_LR_A ** -0.5),
        'rw_g2': nrm((DEPTH, RW_LR_G, GROUP_W), RW_LR_G ** -0.5),
        'rw_k_k': 0.85 + nrm((DEPTH, GROUP_W), 0.02),
        'rw_k_a': 1.0 + nrm((DEPTH, GROUP_W), 0.02),
        'rw_r_k': nrm((DEPTH, RW_HEADS, RW_HEADDIM), 0.1),
        'rw_ln_w': 1.0 + nrm((DEPTH, GROUP_W), 0.02),
        'rw_ln_b': nrm((DEPTH, GROUP_W), 0.02),
        'w_out': nrm((DEPTH, D_MIX, D_MODEL), DN_BETA * D_MIX ** -0.5),
        'ln1_g': 1.0 + nrm((DEPTH, D_MODEL), 0.02),
        'ln1_b': nrm((DEPTH, D_MODEL), 0.02),
        'router_w': nrm((DEPTH, D_MODEL, N_EXPERTS), D_MODEL ** -0.5),
        'router_b': nrm((DEPTH, N_EXPERTS), 0.01),
        'exp_w_gu': nrm((DEPTH, N_EXPERTS, D_MODEL, 2 * D_FF), DN_BETA * D_MODEL ** -0.5),
        'exp_b_gu': nrm((DEPTH, N_EXPERTS, 2 * D_FF), 0.02),
        'exp_w_down': nrm((DEPTH, N_EXPERTS, D_FF, D_MODEL), DN_BETA * D_FF ** -0.5),
        'exp_b_down': nrm((DEPTH, N_EXPERTS, D_MODEL), 0.02),
        'ln2_g': 1.0 + nrm((DEPTH, D_MODEL), 0.02),
        'ln2_b': nrm((DEPTH, D_MODEL), 0.02),
    }


def reference(x_prompt, x_sample, state_ssd_conv, state_ssd, state_mlstm_c, state_mlstm_n, state_mlstm_m,
              state_hgrn, state_rwkv_shift, state_rwkv, w_in, ssd_conv_w, ssd_conv_b, ssd_dt_bias, ssd_a_log,
              ssd_d, ssd_norm_w, ml_b_if, ml_norm_w, hg_lower_bounds, hg_norm_w, rw_mu, rw_w0, rw_w2, rw_a0,
              rw_a2, rw_g2, rw_k_k, rw_k_a, rw_r_k, rw_ln_w, rw_ln_b, w_out, ln1_g, ln1_b, router_w, router_b,
              exp_w_gu, exp_b_gu, exp_w_down, exp_b_down, ln2_g, ln2_b):
    bp, tp = x_prompt.shape[:2]
    bs, ts = x_sample.shape[:2]
    n_p = bp * tp
    caches = (state_ssd_conv, state_ssd, state_mlstm_c, state_mlstm_n, state_mlstm_m,
              state_hgrn, state_rwkv_shift, state_rwkv)
    lbs = jax.nn.softmax(hg_lower_bounds.astype(jnp.float32), axis=0)
    lbs = jnp.cumsum(lbs, axis=0) - lbs[0]
    h = jnp.concatenate([x_prompt.reshape(n_p, D_MODEL), x_sample.reshape(bs * ts, D_MODEL)], axis=0)
    new_p = []
    new_s = []
    for l in range(DEPTH):
        p = (ssd_conv_w[l], ssd_conv_b[l], ssd_dt_bias[l], ssd_a_log[l], ssd_d[l], ssd_norm_w[l],
             ml_b_if[l], ml_norm_w[l], hg_norm_w[l], rw_mu[l], rw_w0[l], rw_w2[l], rw_a0[l], rw_a2[l],
             rw_g2[l], rw_k_k[l], rw_k_a[l], rw_r_k[l], rw_ln_w[l], rw_ln_b[l])
        u = h @ w_in[l]
        st_p = tuple(jnp.zeros((bp,) + c.shape[2:], jnp.float32) for c in caches)
        st_s = tuple(c[l] for c in caches)
        yp, sp = _token_mixers(u[:n_p].reshape(bp, tp, IN_PROJ), st_p, p, lbs[l])
        ys, ss = _token_mixers(u[n_p:].reshape(bs, ts, IN_PROJ), st_s, p, lbs[l])
        mix = jnp.concatenate([yp.reshape(n_p, D_MIX), ys.reshape(bs * ts, D_MIX)], axis=0) @ w_out[l]
        h = _layer_norm(DN_ALPHA * h + mix, ln1_g[l], ln1_b[l])
        ffn = _moe(h, router_w[l], router_b[l], exp_w_gu[l], exp_b_gu[l], exp_w_down[l], exp_b_down[l])
        h = _layer_norm(DN_ALPHA * h + ffn, ln2_g[l], ln2_b[l])
        new_p.append(sp)
        new_s.append(ss)
    ssd_conv_p, ssd_p, mlstm_c_p, mlstm_n_p, mlstm_m_p, hgrn_p, rwkv_shift_p, rwkv_p = [
        jnp.stack([s[i] for s in new_p]) for i in range(N_STATES)]
    ssd_conv_s, ssd_s, mlstm_c_s, mlstm_n_s, mlstm_m_s, hgrn_s, rwkv_shift_s, rwkv_s = [
        jnp.stack([s[i] for s in new_s]) for i in range(N_STATES)]
    y_prompt = h[:n_p].reshape(bp, tp, D_MODEL)
    y_sample = h[n_p:].reshape(bs, ts, D_MODEL)
    return (y_prompt, y_sample, ssd_conv_p, ssd_conv_s, ssd_p, ssd_s, mlstm_c_p, mlstm_c_s, mlstm_n_p, mlstm_n_s,
            mlstm_m_p, mlstm_m_s, hgrn_p, hgrn_s, rwkv_shift_p, rwkv_shift_s, rwkv_p, rwkv_s)
```

```python
import functools

import jax
import jax.numpy as jnp
from jax import lax
from jax.experimental import pallas as pl
from jax.experimental.pallas import tpu as pltpu

F32 = jnp.float32
BF16 = jnp.bfloat16
HI = lax.Precision.HIGHEST
NT_DIMS = (((1,), (1,)), ((), ()))
TN_DIMS = (((0,), (0,)), ((), ()))

GROUP_W = 1024
CHUNK = 64
SSD_HEADDIM, SSD_HEADS, SSD_STATE, SSD_GROUPS, SSD_CONV = 64, 16, 128, 2, 4
SSD_CONV_CH = GROUP_W + 2 * SSD_GROUPS * SSD_STATE
SSD_IN = GROUP_W + SSD_CONV_CH + SSD_HEADS
SSD_INP = 2688
ML_HEADS, ML_DV, ML_DK = 4, 256, 128
ML_IN = 2 * ML_HEADS * ML_DK + GROUP_W + 2 * ML_HEADS + GROUP_W
ML_INP = 3200
HG_HEADS, HG_DK, HG_DV = 8, 128, 128
HG_IN = 4 * GROUP_W
RW_HEADS, RW_HEADDIM = 16, 64
RW_LR_W, RW_LR_A, RW_LR_G = 64, 64, 128
RW_IN = 3 * GROUP_W + RW_LR_W + RW_LR_A + RW_LR_G
RW_LN_EPS = 64e-5
TOP_K = 4
SWIGLU_LIMIT = 7.0
SWIGLU_ALPHA = 1.702
LN_EPS = 1e-5
RMS_EPS = 1e-6
NEG_BIG = -1e30
LANES = 128
MOE_TM = 256
VMEM_LIMIT = 56 * 1024 * 1024


def _cparams(n_axes, vmem=VMEM_LIMIT):
    return pltpu.CompilerParams(dimension_semantics=("arbitrary",) * n_axes, vmem_limit_bytes=vmem)


def _mm(a, b):
    return jnp.dot(a.astype(BF16), b.astype(BF16), preferred_element_type=F32)


def _mm_nt(a, b):
    return lax.dot_general(a.astype(BF16), b.astype(BF16), NT_DIMS, preferred_element_type=F32)


def _mm_tn(a, b):
    return lax.dot_general(a.astype(BF16), b.astype(BF16), TN_DIMS, preferred_element_type=F32)


def _mm_hi(a, b):
    return jnp.dot(a, b, precision=HI, preferred_element_type=F32)


def _tril(n, strict=False):
    r = lax.broadcasted_iota(jnp.int32, (n, n), 0)
    c = lax.broadcasted_iota(jnp.int32, (n, n), 1)
    return (r > c) if strict else (r >= c)


def _cumsum_rows(x):
    return _mm_hi(_tril(x.shape[0]).astype(F32), x)


def _transpose_lanes(x):
    n = x.shape[1]
    eye = (lax.broadcasted_iota(jnp.int32, (n, n), 0) == lax.broadcasted_iota(jnp.int32, (n, n), 1)).astype(F32)
    return lax.dot_general(eye, x, NT_DIMS, precision=HI, preferred_element_type=F32)


def _sigmoid(x):
    return 1.0 / (1.0 + jnp.exp(-x))


def _silu(x):
    return x * _sigmoid(x)


def _softplus(x):
    return jnp.maximum(x, 0.0) + jnp.log1p(jnp.exp(-jnp.abs(x)))


def _layer_norm_rows(x, g, b, eps):
    mu = jnp.mean(x, axis=-1, keepdims=True)
    xc = x - mu
    var = jnp.mean(xc * xc, axis=-1, keepdims=True)
    return xc * lax.rsqrt(var + eps) * g + b


def _matmul_kernel(x_ref, w_ref, o_ref):
    o_ref[...] = jnp.dot(x_ref[...], w_ref[...], preferred_element_type=F32).astype(o_ref.dtype)


def _pick_tile(n, cands):
    for c in cands:
        if n % c == 0:
            return c
    return n


def _matmul(x, w, out_dtype=F32):
    m, k = x.shape
    n = w.shape[1]
    tm = _pick_tile(m, (1088, 1024, 512, 256, 128, 64, 32, 16, 8))
    tn = _pick_tile(n, (512, 256, 128))
    return pl.pallas_call(
        _matmul_kernel,
        grid=(m // tm, n // tn),
        in_specs=[pl.BlockSpec((tm, k), lambda i, j: (i, 0)), pl.BlockSpec((k, tn), lambda i, j: (0, j))],
        out_specs=pl.BlockSpec((tm, tn), lambda i, j: (i, j)),
        out_shape=jax.ShapeDtypeStruct((m, n), out_dtype),
        compiler_params=_cparams(2),
        name="dense_matmul",
    )(x, w)


def _ssd_kernel(L, u_ref, conv0_ref, s0_ref, cw_ref, cb_ref, dtb_ref, alog_ref, dsk_ref, nw_ref,
                y_ref, convo_ref, so_ref, xp_scr, s_scr, y_scr):
    c = pl.program_id(1)

    @pl.when(c == 0)
    def _():
        xp_scr[5:8, :] = conv0_ref[...]
        s_scr[...] = s0_ref[...]

    u = u_ref[...]
    xp_scr[8:8 + L, :] = u[:, GROUP_W:GROUP_W + SSD_CONV_CH]
    conv = xp_scr[5:5 + L, :] * cw_ref[0:1, :]
    for j in range(1, SSD_CONV):
        conv = conv + xp_scr[5 + j:5 + j + L, :] * cw_ref[j:j + 1, :]
    conv = cb_ref[...] + conv
    tail = xp_scr[5 + L:8 + L, :]
    xp_scr[5:8, :] = tail
    convo_ref[...] = tail

    xbc = _silu(conv)
    xh = xbc[:, :GROUP_W]
    bm = xbc[:, GROUP_W:GROUP_W + SSD_GROUPS * SSD_STATE]
    cm = xbc[:, GROUP_W + SSD_GROUPS * SSD_STATE:]
    dtp = _softplus(u[:, GROUP_W + SSD_CONV_CH:] + dtb_ref[...])
    lap = dtp * (-jnp.exp(alog_ref[...]))
    cum = _cumsum_rows(lap)
    cum_t = _transpose_lanes(cum)
    dt_t = _transpose_lanes(dtp)
    cumlast = cum[L - 1:L, :]
    mask = _tril(L)
    rep = SSD_HEADS // SSD_GROUPS
    for g in range(SSD_GROUPS):
        bg = bm[:, g * SSD_STATE:(g + 1) * SSD_STATE]
        cg = cm[:, g * SSD_STATE:(g + 1) * SSD_STATE]
        cb = _mm_nt(cg, bg)
        for r8 in range(rep):
            r = g * rep + r8
            col = cum[:, r:r + 1]
            row = cum_t[r:r + 1, :]
            dec = jnp.where(mask, jnp.exp(jnp.where(mask, col - row, 0.0)), 0.0)
            w = cb * dec * dt_t[r:r + 1, :]
            xr = xh[:, r * SSD_HEADDIM:(r + 1) * SSD_HEADDIM]
            sr = s_scr[r]
            y = _mm(w, xr) + _mm_nt(cg, sr) * jnp.exp(col)
            cl = cumlast[:, r:r + 1]
            wend = jnp.exp(cl - col) * dtp[:, r:r + 1]
            s_scr[r] = sr * jnp.exp(cl) + _mm_tn(wend * xr, bg)
            y = y + dsk_ref[:, r:r + 1] * xr
            y_scr[:, r * SSD_HEADDIM:(r + 1) * SSD_HEADDIM] = y

    yz = y_scr[...] * _silu(u[:, :GROUP_W])
    gw = GROUP_W // SSD_GROUPS
    for g in range(SSD_GROUPS):
        yg = yz[:, g * gw:(g + 1) * gw]
        ms = jnp.mean(yg * yg, axis=-1, keepdims=True)
        y_ref[:, g * gw:(g + 1) * gw] = (yg * lax.rsqrt(ms + RMS_EPS) * nw_ref[:, g * gw:(g + 1) * gw]).astype(y_ref.dtype)

    @pl.when(c == pl.num_programs(1) - 1)
    def _():
        so_ref[...] = s_scr[...]


def _ssd_mixer(u2d, row0, b, t, conv0, s0, cw, cb, dtb, alog, dsk, nw):
    L = min(t, CHUNK)
    nc = t // L
    blk0 = row0 // L
    const = lambda shape: pl.BlockSpec(shape, lambda i, c: (0,) * len(shape))
    return pl.pallas_call(
        functools.partial(_ssd_kernel, L),
        grid=(b, nc),
        in_specs=[
            pl.BlockSpec((L, SSD_INP), lambda i, c: (blk0 + i * nc + c, 0)),
            pl.BlockSpec((None, SSD_CONV - 1, SSD_CONV_CH), lambda i, c: (i, 0, 0)),
            pl.BlockSpec((None, SSD_HEADS, SSD_HEADDIM, SSD_STATE), lambda i, c: (i, 0, 0, 0)),
            const((SSD_CONV, SSD_CONV_CH)), const((1, SSD_CONV_CH)), const((1, LANES)), const((1, LANES)),
            const((1, LANES)), const((1, GROUP_W)),
        ],
        out_specs=[
            pl.BlockSpec((L, GROUP_W), lambda i, c: (i * nc + c, 0)),
            pl.BlockSpec((None, SSD_CONV - 1, SSD_CONV_CH), lambda i, c: (i, 0, 0)),
            pl.BlockSpec((None, SSD_HEADS, SSD_HEADDIM, SSD_STATE), lambda i, c: (i, 0, 0, 0)),
        ],
        out_shape=[
            jax.ShapeDtypeStruct((b * t, GROUP_W), BF16),
            jax.ShapeDtypeStruct((b, SSD_CONV - 1, SSD_CONV_CH), F32),
            jax.ShapeDtypeStruct((b, SSD_HEADS, SSD_HEADDIM, SSD_STATE), F32),
        ],
        scratch_shapes=[pltpu.VMEM((L + 8, SSD_CONV_CH), F32), pltpu.VMEM((SSD_HEADS, SSD_HEADDIM, SSD_STATE), F32),
                        pltpu.VMEM((L, GROUP_W), F32)],
        compiler_params=_cparams(2),
        name="ssd_mixer",
    )(u2d, conv0, s0, cw, cb, dtb, alog, dsk, nw)


ML_QK = ML_HEADS * ML_DK
ML_GATE_OFF = 2 * ML_QK + GROUP_W
ML_O_OFF = ML_GATE_OFF + LANES


def _mlstm_kernel(L, u_ref, c0_ref, n0_ref, m0_ref, bif_ref, nw_ref, y_ref, co_ref, no_ref, mo_ref,
                  c_scr, n_scr, m_scr):
    c = pl.program_id(1)

    @pl.when(c == 0)
    def _():
        c_scr[...] = c0_ref[...]
        n_scr[...] = n0_ref[...]
        m_scr[...] = m0_ref[...]

    u = u_ref[...]
    gates = u[:, ML_GATE_OFF:ML_GATE_OFF + LANES] + bif_ref[...]
    lf = -_softplus(-gates)
    bcum = _cumsum_rows(lf)
    b_t = _transpose_lanes(bcum)
    g_t = _transpose_lanes(gates)
    mrow = m_scr[...]
    mask = _tril(L)
    for h in range(ML_HEADS):
        q = u[:, h * ML_DK:(h + 1) * ML_DK] * (ML_DK ** -0.5)
        k = u[:, ML_QK + h * ML_DK:ML_QK + (h + 1) * ML_DK]
        v = u[:, 2 * ML_QK + h * ML_DV:2 * ML_QK + (h + 1) * ML_DV]
        fh = ML_HEADS + h
        bcol = bcum[:, fh:fh + 1]
        brow = b_t[fh:fh + 1, :]
        irow = g_t[h:h + 1, :]
        icol = gates[:, h:h + 1]
        m_h = mrow[:, h:h + 1]
        d = jnp.where(mask, bcol - brow + irow, NEG_BIG)
        gg = bcol + m_h
        mt = jnp.maximum(gg, jnp.max(d, axis=1, keepdims=True))
        wts = jnp.where(mask, jnp.exp(d - mt), 0.0)
        inter = jnp.exp(gg - mt)
        qk = _mm_nt(q, k) * wts
        c_h = c_scr[h]
        n_h = n_scr[h:h + 1, :]
        num = _mm(qk, v) + inter * _mm_nt(q, c_h)
        den = jnp.sum(qk, axis=1, keepdims=True) + inter * jnp.sum(q * n_h, axis=1, keepdims=True)
        hh = num / jnp.maximum(jnp.abs(den), jnp.exp(-mt))
        bl = bcum[L - 1:L, fh:fh + 1]
        loc = bl - bcol + icol
        m_new = jnp.maximum(bl + m_h, jnp.max(loc, axis=0, keepdims=True))
        wl = jnp.exp(loc - m_new)
        decay = jnp.exp(bl + m_h - m_new)
        c_scr[h] = decay * c_h + _mm_tn(wl * v, k)
        n_scr[h:h + 1, :] = decay * n_h + jnp.sum(wl * k, axis=0, keepdims=True)
        m_scr[:, h:h + 1] = m_new
        ms = jnp.mean(hh * hh, axis=-1, keepdims=True)
        hn = hh * lax.rsqrt(ms + RMS_EPS) * nw_ref[:, h * ML_DV:(h + 1) * ML_DV]
        o = u[:, ML_O_OFF + h * ML_DV:ML_O_OFF + (h + 1) * ML_DV]
        y_ref[:, h * ML_DV:(h + 1) * ML_DV] = (hn * _sigmoid(o)).astype(y_ref.dtype)

    @pl.when(c == pl.num_programs(1) - 1)
    def _():
        co_ref[...] = c_scr[...]
        no_ref[...] = n_scr[...]
        mo_ref[...] = m_scr[...]


def _mlstm_mixer(u2d, row0, b, t, c0, n0, m0p, bif, nw):
    L = min(t, CHUNK)
    nc = t // L
    blk0 = row0 // L
    const = lambda shape: pl.BlockSpec(shape, lambda i, c: (0,) * len(shape))
    return pl.pallas_call(
        functools.partial(_mlstm_kernel, L),
        grid=(b, nc),
        in_specs=[
            pl.BlockSpec((L, ML_INP), lambda i, c: (blk0 + i * nc + c, 0)),
            pl.BlockSpec((None, ML_HEADS, ML_DV, ML_DK), lambda i, c: (i, 0, 0, 0)),
            pl.BlockSpec((None, ML_HEADS, ML_DK), lambda i, c: (i, 0, 0)),
            pl.BlockSpec((None, 1, LANES), lambda i, c: (i, 0, 0)),
            const((1, LANES)), const((1, GROUP_W)),
        ],
        out_specs=[
            pl.BlockSpec((L, GROUP_W), lambda i, c: (i * nc + c, 0)),
            pl.BlockSpec((None, ML_HEADS, ML_DV, ML_DK), lambda i, c: (i, 0, 0, 0)),
            pl.BlockSpec((None, ML_HEADS, ML_DK), lambda i, c: (i, 0, 0)),
            pl.BlockSpec((None, 1, LANES), lambda i, c: (i, 0, 0)),
        ],
        out_shape=[
            jax.ShapeDtypeStruct((b * t, GROUP_W), BF16),
            jax.ShapeDtypeStruct((b, ML_HEADS, ML_DV, ML_DK), F32),
            jax.ShapeDtypeStruct((b, ML_HEADS, ML_DK), F32),
            jax.ShapeDtypeStruct((b, 1, LANES), F32),
        ],
        scratch_shapes=[pltpu.VMEM((ML_HEADS, ML_DV, ML_DK), F32), pltpu.VMEM((ML_HEADS, ML_DK), F32),
                        pltpu.VMEM((1, LANES), F32)],
        compiler_params=_cparams(2),
        name="mlstm_mixer",
    )(u2d, c0, n0, m0p, bif, nw)


def _gla_kernel(L, u_ref, s0_ref, lb_ref, nw_ref, y_ref, so_ref, s_scr, g_scr, k_scr):
    c = pl.program_id(1)

    @pl.when(c == 0)
    def _():
        s_scr[...] = s0_ref[...]

    u = u_ref[...]
    lb = lb_ref[...]
    q = _silu(u[:, :GROUP_W])
    fp = u[:, GROUP_W:2 * GROUP_W]
    f = lb + (1.0 - lb) * _sigmoid(fp)
    k = (1.0 - lb) * _sigmoid(-fp)
    v = u[:, 2 * GROUP_W:3 * GROUP_W]
    gate = u[:, 3 * GROUP_W:]
    gc = _cumsum_rows(jnp.log(f))
    g_scr[...] = gc
    k_scr[...] = k
    rows = lax.broadcasted_iota(jnp.int32, (L, 1), 0)
    lanes = lax.broadcasted_iota(jnp.int32, (L, L), 1)
    for h in range(HG_HEADS):
        sl = slice(h * HG_DK, (h + 1) * HG_DK)
        gh, qh, kh, vh = gc[:, sl], q[:, sl], k[:, sl], v[:, sl]

        def body(sb, a_mat, gh=gh, qh=qh, sl=sl):
            base = pl.multiple_of(sb * 8, 8)
            g8 = g_scr[pl.ds(base, 8), sl]
            k8 = k_scr[pl.ds(base, 8), sl]
            for j in range(8):
                s = base + j
                m = rows >= s
                e = jnp.where(m, jnp.exp(jnp.where(m, gh - g8[j:j + 1, :], 0.0)), 0.0)
                a = jnp.sum(qh * e * k8[j:j + 1, :], axis=1, keepdims=True)
                a_mat = jnp.where(lanes == s, a, a_mat)
            return a_mat

        a_mat = lax.fori_loop(0, L // 8, body, jnp.zeros((L, L), F32))
        st = s_scr[h]
        o = _mm(a_mat, vh) + _mm_nt(qh * jnp.exp(gh), st)
        gl = gc[L - 1:L, sl]
        s_scr[h] = st * jnp.exp(gl) + _mm_tn(vh, kh * jnp.exp(gl - gh))
        ms = jnp.mean(o * o, axis=-1, keepdims=True)
        on = o * lax.rsqrt(ms + RMS_EPS) * nw_ref[:, sl]
        y_ref[:, sl] = (on * _silu(gate[:, sl])).astype(y_ref.dtype)

    @pl.when(c == pl.num_programs(1) - 1)
    def _():
        so_ref[...] = s_scr[...]


def _gla_mixer(u2d, row0, b, t, s0t, lb, nw):
    L = min(t, CHUNK)
    nc = t // L
    blk0 = row0 // L
    const = lambda shape: pl.BlockSpec(shape, lambda i, c: (0,) * len(shape))
    return pl.pallas_call(
        functools.partial(_gla_kernel, L),
        grid=(b, nc),
        in_specs=[
            pl.BlockSpec((L, HG_IN), lambda i, c: (blk0 + i * nc + c, 0)),
            pl.BlockSpec((None, HG_HEADS, HG_DV, HG_DK), lambda i, c: (i, 0, 0, 0)),
            const((1, GROUP_W)), const((1, GROUP_W)),
        ],
        out_specs=[
            pl.BlockSpec((L, GROUP_W), lambda i, c: (i * nc + c, 0)),
            pl.BlockSpec((None, HG_HEADS, HG_DV, HG_DK), lambda i, c: (i, 0, 0, 0)),
        ],
        out_shape=[
            jax.ShapeDtypeStruct((b * t, GROUP_W), BF16),
            jax.ShapeDtypeStruct((b, HG_HEADS, HG_DV, HG_DK), F32),
        ],
        scratch_shapes=[pltpu.VMEM((HG_HEADS, HG_DV, HG_DK), F32), pltpu.VMEM((L, GROUP_W), F32),
                        pltpu.VMEM((L, GROUP_W), F32)],
        compiler_params=_cparams(2),
        name="gla_mixer",
    )(u2d, s0t, lb, nw)


RW_LORA_OFF = 3 * GROUP_W


def _rw_prep_kernel(tr, u_ref, sh0_ref, mu_ref, w0_ref, w2p_ref, a0_ref, a2p_ref, g2_ref, kk_ref, ka_ref,
                    r_o, w_o, k_o, v_o, kk_o, a_o, g_o, sh_o, buf):
    c = pl.program_id(1)

    @pl.when(c == 0)
    def _():
        buf[7:8, :] = sh0_ref[...]

    u = u_ref[...]
    buf[8:8 + tr, :] = u
    prev = buf[7:7 + tr, :]
    last = u[tr - 1:tr, :]
    buf[7:8, :] = last
    sh_o[...] = last
    um = u + (prev - u) * mu_ref[...]
    r = um[:, :GROUP_W]
    k = um[:, GROUP_W:2 * GROUP_W]
    v = um[:, 2 * GROUP_W:3 * GROUP_W]
    xwa = um[:, RW_LORA_OFF:RW_LORA_OFF + LANES]
    xg = um[:, RW_LORA_OFF + LANES:]
    w_log = -_softplus(-(w0_ref[...] + _mm(jnp.tanh(xwa), w2p_ref[...]))) - 0.5
    a = _sigmoid(a0_ref[...] + _mm(xwa, a2p_ref[...]))
    r_o[...] = r
    w_o[...] = jnp.exp(-jnp.exp(w_log))
    k_o[...] = k * (1.0 + (a - 1.0) * ka_ref[...])
    v_o[...] = v
    kk_o[...] = k * kk_ref[...]
    a_o[...] = a
    g_o[...] = _mm(_sigmoid(xg), g2_ref[...])


def _rw_prep(u2d, row0, b, t, sh0, mu, w0, w2p, a0, a2p, g2, kk, ka):
    tr = _pick_tile(t, (256, 128, 64, 32, 16, 8))
    nc = t // tr
    blk0 = row0 // tr
    const = lambda shape: pl.BlockSpec(shape, lambda i, c: (0,) * len(shape))
    row_out = pl.BlockSpec((tr, GROUP_W), lambda i, c: (i * nc + c, 0))
    return pl.pallas_call(
        functools.partial(_rw_prep_kernel, tr),
        grid=(b, nc),
        in_specs=[
            pl.BlockSpec((tr, RW_IN), lambda i, c: (blk0 + i * nc + c, 0)),
            pl.BlockSpec((None, 1, RW_IN), lambda i, c: (i, 0, 0)),
            const((1, RW_IN)), const((1, GROUP_W)), const((LANES, GROUP_W)), const((1, GROUP_W)),
            const((LANES, GROUP_W)), const((RW_LR_G, GROUP_W)), const((1, GROUP_W)), const((1, GROUP_W)),
        ],
        out_specs=[row_out] * 7 + [pl.BlockSpec((None, 1, RW_IN), lambda i, c: (i, 0, 0))],
        out_shape=[jax.ShapeDtypeStruct((b * t, GROUP_W), F32)] * 7 + [jax.ShapeDtypeStruct((b, 1, RW_IN), F32)],
        scratch_shapes=[pltpu.VMEM((tr + 8, RW_IN), F32)],
        compiler_params=_cparams(2),
        name="rwkv_prep",
    )(u2d, sh0, mu, w0, w2p, a0, a2p, g2, kk, ka)


def _rw_scan_kernel(tc, r_ref, w_ref, k_ref, v_ref, kk_ref, a_ref, g_ref, s0_ref, rk_ref, lnw_ref, lnb_ref,
                    y_ref, so_ref, s_scr):
    c = pl.program_id(1)

    @pl.when(c == 0)
    def _():
        s_scr[...] = s0_ref[...]

    shp = (RW_HEADDIM, RW_HEADS, RW_HEADDIM)
    eye = lax.broadcasted_iota(jnp.int32, shp, 0) == lax.broadcasted_iota(jnp.int32, shp, 2)
    rk = rk_ref[...]
    lnw = lnw_ref[...]
    lnb = lnb_ref[...]

    def step(t, carry):
        r, w, k, v, kk, a, g = r_ref[t], w_ref[t], k_ref[t], v_ref[t], kk_ref[t], a_ref[t], g_ref[t]
        kk = kk / jnp.maximum(jnp.sqrt(jnp.sum(kk * kk, axis=-1, keepdims=True)), 1e-12)
        s = s_scr[...]
        sa = jnp.sum(s * (-kk)[None], axis=-1, keepdims=True)
        vcol = jnp.sum(jnp.where(eye, v[None], 0.0), axis=-1, keepdims=True)
        s = s * w[None] + sa * (kk * a)[None] + vcol * k[None]
        s_scr[...] = s
        ycol = jnp.sum(s * r[None], axis=-1, keepdims=True)
        y = jnp.sum(jnp.where(eye, ycol, 0.0), axis=0)
        y = _layer_norm_rows(y, lnw, lnb, RW_LN_EPS)
        y = y + jnp.sum(r * k * rk, axis=-1, keepdims=True) * v
        y_ref[t] = y * g
        return carry

    lax.fori_loop(0, tc, step, 0)

    @pl.when(c == pl.num_programs(1) - 1)
    def _():
        so_ref[...] = s_scr[...]


def _rw_scan(seqs, b, t, s0t, rk, lnw, lnb):
    tc = _pick_tile(t, (128, 64, 32, 16, 8))
    nc = t // tc
    seq_spec = pl.BlockSpec((None, tc, RW_HEADS, RW_HEADDIM), lambda i, c: (i, c, 0, 0))
    st_spec = pl.BlockSpec((None, RW_HEADDIM, RW_HEADS, RW_HEADDIM), lambda i, c: (i, 0, 0, 0))
    par_spec = pl.BlockSpec((RW_HEADS, RW_HEADDIM), lambda i, c: (0, 0))
    seqs4 = [x.reshape(b, t, RW_HEADS, RW_HEADDIM) for x in seqs]
    return pl.pallas_call(
        functools.partial(_rw_scan_kernel, tc),
        grid=(b, nc),
        in_specs=[seq_spec] * 7 + [st_spec, par_spec, par_spec, par_spec],
        out_specs=[seq_spec, st_spec],
        out_shape=[jax.ShapeDtypeStruct((b, t, RW_HEADS, RW_HEADDIM), F32),
                   jax.ShapeDtypeStruct((b, RW_HEADDIM, RW_HEADS, RW_HEADDIM), F32)],
        scratch_shapes=[pltpu.VMEM((RW_HEADDIM, RW_HEADS, RW_HEADDIM), F32)],
        compiler_params=_cparams(2),
        name="rwkv_scan",
    )(*seqs4, s0t, rk, lnw, lnb)


def _ln_router_kernel(alpha, h_ref, mix_ref, g_ref, b_ref, rw_ref, rb_ref,
                      h1_ref, idx_ref, gate_ref, rank_ref, cnt_ref, cnt_scr):
    i = pl.program_id(0)

    @pl.when(i == 0)
    def _():
        cnt_scr[...] = jnp.zeros_like(cnt_scr)

    h1 = _layer_norm_rows(alpha * h_ref[...] + mix_ref[...], g_ref[...], b_ref[...], LN_EPS)
    h1_ref[...] = h1
    tm = h1.shape[0]
    logits = _mm_hi(h1, rw_ref[...]) + rb_ref[...]
    lane = lax.broadcasted_iota(jnp.int32, (tm, LANES), 1)
    lg = logits
    vals, sels = [], []
    idx_out = jnp.zeros((tm, LANES), jnp.int32)
    for k in range(TOP_K):
        m = jnp.max(lg, axis=-1, keepdims=True)
        idx = jnp.min(jnp.where(lg == m, lane, LANES), axis=-1, keepdims=True)
        sel = lane == idx
        lg = jnp.where(sel, -3e38, lg)
        vals.append(m)
        sels.append(sel)
        idx_out = jnp.where(lane == k, idx, idx_out)
    es = [jnp.exp(v - vals[0]) for v in vals]
    den = es[0] + es[1] + es[2] + es[3]
    gate_out = jnp.zeros((tm, LANES), F32)
    multi = jnp.zeros((tm, LANES), F32)
    for k in range(TOP_K):
        gate_out = jnp.where(lane == k, es[k] / den, gate_out)
        multi = multi + sels[k].astype(F32)
    before = _mm(_tril(tm, strict=True).astype(F32), multi) + cnt_scr[...]
    rank_out = jnp.zeros((tm, LANES), jnp.int32)
    for k in range(TOP_K):
        rk = jnp.sum(jnp.where(sels[k], before, 0.0), axis=-1, keepdims=True)
        rank_out = jnp.where(lane == k, rk.astype(jnp.int32), rank_out)
    idx_ref[...] = idx_out
    gate_ref[...] = gate_out
    rank_ref[...] = rank_out
    cnt_scr[...] = cnt_scr[...] + jnp.sum(multi, axis=0, keepdims=True)
    cnt_ref[...] = cnt_scr[...]


def _ln_router(alpha, h, mix, g, b, rwp, rbp):
    n, d = h.shape
    tm = _pick_tile(n, (256, 128, 64, 32, 16, 8))
    row = lambda w: pl.BlockSpec((tm, w), lambda i: (i, 0))
    const = lambda shape: pl.BlockSpec(shape, lambda i: (0,) * len(shape))
    return pl.pallas_call(
        functools.partial(_ln_router_kernel, alpha),
        grid=(n // tm,),
        in_specs=[row(d), row(d), const((1, d)), const((1, d)), const((d, LANES)), const((1, LANES))],
        out_specs=[row(d), row(LANES), row(LANES), row(LANES), const((1, LANES))],
        out_shape=[jax.ShapeDtypeStruct((n, d), F32),
                   jax.ShapeDtypeStruct((n, LANES), jnp.int32), jax.ShapeDtypeStruct((n, LANES), F32),
                   jax.ShapeDtypeStruct((n, LANES), jnp.int32), jax.ShapeDtypeStruct((1, LANES), F32)],
        scratch_shapes=[pltpu.VMEM((1, LANES), F32)],
        compiler_params=_cparams(1),
        name="ln_router",
    )(h, mix, g, b, rwp, rbp)


def _dispatch_kernel(td, pos_ref, h_ref, xs_in_ref, xs_ref, sem):
    del xs_in_ref

    def row_copy(r, p):
        return pltpu.make_async_copy(h_ref.at[pl.ds(r, 1)], xs_ref.at[pl.ds(p, 1)], sem)

    def issue(r, carry):
        for k in range(TOP_K):
            row_copy(r, pos_ref[0, r * TOP_K + k]).start()
        return carry

    def drain(r, carry):
        for k in range(TOP_K):
            row_copy(r, pos_ref[0, r * TOP_K + k]).wait()
        return carry

    lax.fori_loop(0, td, issue, 0)
    lax.fori_loop(0, td, drain, 0)


def _dispatch(h1, pos, p_rows):
    n, d = h1.shape
    td = _pick_tile(n, (128, 64, 32, 16, 8))
    pos2 = pos.reshape(n // td, 1, td * TOP_K)
    xs0 = jnp.zeros((p_rows, d), F32)
    return pl.pallas_call(
        functools.partial(_dispatch_kernel, td),
        grid=(n // td,),
        in_specs=[pl.BlockSpec((None, 1, td * TOP_K), lambda i: (i, 0, 0), memory_space=pltpu.SMEM),
                  pl.BlockSpec((td, d), lambda i: (i, 0)),
                  pl.BlockSpec(memory_space=pl.ANY)],
        out_specs=pl.BlockSpec(memory_space=pl.ANY),
        out_shape=jax.ShapeDtypeStruct((p_rows, d), F32),
        scratch_shapes=[pltpu.SemaphoreType.DMA],
        input_output_aliases={2: 0},
        compiler_params=_cparams(1),
        name="moe_dispatch",
    )(pos2, h1, xs0)


def _gemm1_kernel(se, sn, so, sm, sf, sv, x_ref, wg_ref, wu_ref, bg_ref, bu_ref, o_ref, wgb, wub):
    s = pl.program_id(0)

    @pl.when(sf[s] == 1)
    def _():
        wgb[...] = wg_ref[...].astype(BF16)
        wub[...] = wu_ref[...].astype(BF16)

    @pl.when(sv[s] == 1)
    def _():
        x = x_ref[...]
        hg = jnp.dot(x, wgb[...], preferred_element_type=F32) + bg_ref[...]
        hu = jnp.dot(x, wub[...], preferred_element_type=F32) + bu_ref[...]
        hg = jnp.minimum(hg, SWIGLU_LIMIT)
        hu = jnp.clip(hu, -SWIGLU_LIMIT, SWIGLU_LIMIT)
        o_ref[...] = (hg * _sigmoid(SWIGLU_ALPHA * hg) * (hu + 1.0)).astype(o_ref.dtype)

    @pl.when(sv[s] == 0)
    def _():
        o_ref[...] = jnp.zeros_like(o_ref)


def _gemm1(sched, xs, w_gu, b_gu, tn):
    p_rows, d = xs.shape
    n_exp, _, f2 = w_gu.shape
    ff = f2 // 2
    nb = ff // tn
    n_steps = sched[0].shape[0]
    b3 = b_gu.reshape(n_exp, 1, f2)
    grid_spec = pltpu.PrefetchScalarGridSpec(
        num_scalar_prefetch=6,
        grid=(n_steps,),
        in_specs=[
            pl.BlockSpec((MOE_TM, d), lambda s, se, sn, so, sm, sf, sv: (sm[s], 0)),
            pl.BlockSpec((None, d, tn), lambda s, se, sn, so, sm, sf, sv: (se[s], 0, sn[s])),
            pl.BlockSpec((None, d, tn), lambda s, se, sn, so, sm, sf, sv: (se[s], 0, nb + sn[s])),
            pl.BlockSpec((None, 1, tn), lambda s, se, sn, so, sm, sf, sv: (se[s], 0, sn[s])),
            pl.BlockSpec((None, 1, tn), lambda s, se, sn, so, sm, sf, sv: (se[s], 0, nb + sn[s])),
        ],
        out_specs=pl.BlockSpec((MOE_TM, tn), lambda s, se, sn, so, sm, sf, sv: (sm[s], so[s])),
        scratch_shapes=[pltpu.VMEM((d, tn), BF16), pltpu.VMEM((d, tn), BF16)],
    )
    return pl.pallas_call(
        _gemm1_kernel,
        grid_spec=grid_spec,
        out_shape=jax.ShapeDtypeStruct((p_rows, ff), BF16),
        compiler_params=_cparams(1),
        name="moe_gate_up",
    )(*sched, xs, w_gu, w_gu, b3, b3)


def _gemm2_kernel(se, sn, so, sm, sf, sv, x_ref, w_ref, b_ref, o_ref, wb):
    s = pl.program_id(0)

    @pl.when(sf[s] == 1)
    def _():
        wb[...] = w_ref[...].astype(BF16)

    @pl.when(sv[s] == 1)
    def _():
        o_ref[...] = jnp.dot(x_ref[...], wb[...], preferred_element_type=F32) + b_ref[...]

    @pl.when(sv[s] == 0)
    def _():
        o_ref[...] = jnp.zeros_like(o_ref)


def _gemm2(sched, act, w_down, b_down, tn):
    p_rows, ff = act.shape
    n_exp, _, d = w_down.shape
    n_steps = sched[0].shape[0]
    b3 = b_down.reshape(n_exp, 1, d)
    grid_spec = pltpu.PrefetchScalarGridSpec(
        num_scalar_prefetch=6,
        grid=(n_steps,),
        in_specs=[
            pl.BlockSpec((MOE_TM, ff), lambda s, se, sn, so, sm, sf, sv: (sm[s], 0)),
            pl.BlockSpec((None, ff, tn), lambda s, se, sn, so, sm, sf, sv: (se[s], 0, sn[s])),
            pl.BlockSpec((None, 1, tn), lambda s, se, sn, so, sm, sf, sv: (se[s], 0, sn[s])),
        ],
        out_specs=pl.BlockSpec((MOE_TM, tn), lambda s, se, sn, so, sm, sf, sv: (sm[s], so[s])),
        scratch_shapes=[pltpu.VMEM((ff, tn), BF16)],
    )
    return pl.pallas_call(
        _gemm2_kernel,
        grid_spec=grid_spec,
        out_shape=jax.ShapeDtypeStruct((p_rows, d), F32),
        compiler_params=_cparams(1),
        name="moe_down",
    )(*sched, act, w_down, b3)


def _moe_schedule(counts, n_blocks, max_tiles):
    n_exp = counts.shape[0]
    tiles = (counts + MOE_TM - 1) // MOE_TM
    tile_start = jnp.cumsum(tiles) - tiles
    steps_e = tiles * n_blocks
    step_end = jnp.cumsum(steps_e)
    step_start = step_end - steps_e
    total = step_end[-1]
    s = jnp.arange(max_tiles * n_blocks, dtype=jnp.int32)
    valid = s < total
    sc = jnp.minimum(s, jnp.maximum(total - 1, 0))
    e = jnp.minimum(jnp.searchsorted(step_end, sc, side="right"), n_exp - 1).astype(jnp.int32)
    local = sc - step_start[e]
    te = jnp.maximum(tiles[e], 1)
    j = local // te
    i = local % te
    first = jnp.logical_and(valid, i == 0)
    spare = jnp.maximum(s - total, 0)
    jo = jnp.where(valid, j, spare % n_blocks)
    mt = jnp.where(valid, tile_start[e] + i, jnp.sum(tiles) + spare // n_blocks)
    return (e, j.astype(jnp.int32), jo.astype(jnp.int32), mt.astype(jnp.int32), first.astype(jnp.int32),
            valid.astype(jnp.int32))


def _combine_kernel(alpha, tc, pos_ref, gate_ref, h1_ref, g_ref, b_ref, ys_ref, h2_ref, h2b_ref, buf, sem):
    def row_copy(r, k):
        return pltpu.make_async_copy(ys_ref.at[pl.ds(pos_ref[0, r * TOP_K + k], 1)], buf.at[k, pl.ds(r, 1)], sem)

    def issue(r, carry):
        for k in range(TOP_K):
            row_copy(r, k).start()
        return carry

    def drain(r, carry):
        for k in range(TOP_K):
            row_copy(r, k).wait()
        return carry

    lax.fori_loop(0, tc, issue, 0)
    lax.fori_loop(0, tc, drain, 0)
    gates = gate_ref[...]
    ffn = gates[:, 0:1] * buf[0]
    for k in range(1, TOP_K):
        ffn = ffn + gates[:, k:k + 1] * buf[k]
    h2 = _layer_norm_rows(alpha * h1_ref[...] + ffn, g_ref[...], b_ref[...], LN_EPS)
    h2_ref[...] = h2
    h2b_ref[...] = h2.astype(BF16)


def _combine(alpha, pos, gates, h1, g, b, ys):
    n, d = h1.shape
    tc = _pick_tile(n, (64, 32, 16, 8))
    pos2 = pos.reshape(n // tc, 1, tc * TOP_K)
    row = lambda w: pl.BlockSpec((tc, w), lambda i: (i, 0))
    const = lambda shape: pl.BlockSpec(shape, lambda i: (0,) * len(shape))
    return pl.pallas_call(
        functools.partial(_combine_kernel, alpha, tc),
        grid=(n // tc,),
        in_specs=[pl.BlockSpec((None, 1, tc * TOP_K), lambda i: (i, 0, 0), memory_space=pltpu.SMEM),
                  row(LANES), row(d), const((1, d)), const((1, d)), pl.BlockSpec(memory_space=pl.ANY)],
        out_specs=[row(d), row(d)],
        out_shape=[jax.ShapeDtypeStruct((n, d), F32), jax.ShapeDtypeStruct((n, d), BF16)],
        scratch_shapes=[pltpu.VMEM((TOP_K, tc, d), F32), pltpu.SemaphoreType.DMA],
        compiler_params=_cparams(1),
        name="moe_combine",
    )(pos2, gates, h1, g, b, ys)


def _moe_layer(alpha, h, mix, ln1_g, ln1_b, router_w, router_b, w_gu, b_gu, w_down, b_down, ln2_g, ln2_b):
    n, d = h.shape
    n_exp = router_w.shape[1]
    ff = w_down.shape[1]
    rwp = jnp.pad(router_w, ((0, 0), (0, LANES - n_exp)))
    rbp = jnp.pad(router_b.reshape(1, n_exp), ((0, 0), (0, LANES - n_exp)), constant_values=NEG_BIG)
    h1, idx, gates, rank, cnt = _ln_router(alpha, h, mix, ln1_g.reshape(1, d), ln1_b.reshape(1, d), rwp, rbp)
    counts = cnt[0, :n_exp].astype(jnp.int32)
    padded = ((counts + MOE_TM - 1) // MOE_TM) * MOE_TM
    offsets = jnp.cumsum(padded) - padded
    pos = offsets[idx[:, :TOP_K]] + rank[:, :TOP_K]
    max_tiles = (n * TOP_K) // MOE_TM + n_exp
    p_rows = max_tiles * MOE_TM
    xs = _dispatch(h1, pos, p_rows).astype(BF16)
    tn1 = _pick_tile(ff, (512, 256, 128))
    tn2 = _pick_tile(d, (1024, 512, 256, 128))
    act = _gemm1(_moe_schedule(counts, ff // tn1, max_tiles), xs, w_gu, b_gu, tn1)
    ys = _gemm2(_moe_schedule(counts, d // tn2, max_tiles), act, w_down, b_down, tn2)
    return _combine(alpha, pos, gates, h1, ln2_g.reshape(1, d), ln2_b.reshape(1, d), ys)


def _pad_lanes(x, width=LANES, value=0.0):
    return jnp.pad(x.reshape(1, -1), ((0, 0), (0, width - x.size)), constant_values=value)


def _token_mixers(us, row0, b, t, st, p):
    u_ssd, u_ml, u_hg, u_rw = us
    ssd_conv, ssd_s, ml_c, ml_n, ml_m, hg_s, rw_shift, rw_s = st
    ya, conv_new, ssd_new = _ssd_mixer(u_ssd, row0, b, t, ssd_conv, ssd_s, p["conv_w"], p["conv_b"], p["dt_bias"],
                                       p["a_log"], p["d_skip"], p["ssd_nw"])
    yb, c_new, n_new, m_new = _mlstm_mixer(u_ml, row0, b, t, ml_c, ml_n,
                                           jnp.pad(ml_m, ((0, 0), (0, LANES - ML_HEADS))).reshape(b, 1, LANES),
                                           p["ml_bif"], p["ml_nw"])
    yc, hg_new = _gla_mixer(u_hg, row0, b, t, jnp.swapaxes(hg_s, -1, -2), p["hg_lb"], p["hg_nw"])
    prep = _rw_prep(u_rw, row0, b, t, rw_shift.reshape(b, 1, RW_IN), p["rw_mu"], p["rw_w0"], p["rw_w2p"], p["rw_a0"],
                    p["rw_a2p"], p["rw_g2"], p["rw_kk"], p["rw_ka"])
    yd, rw_new = _rw_scan(prep[:7], b, t, jnp.transpose(rw_s, (0, 2, 1, 3)), p["rw_rk"], p["rw_lnw"], p["rw_lnb"])
    y = jnp.concatenate([ya, yb, yc, yd.reshape(b * t, GROUP_W).astype(BF16)], axis=-1)
    new = (conv_new, ssd_new, c_new, n_new, m_new[:, 0, :ML_HEADS], jnp.swapaxes(hg_new, -1, -2),
           prep[7].reshape(b, RW_IN), jnp.transpose(rw_new, (0, 2, 1, 3)))
    return y, new


def _split_w_in(w):
    o1 = SSD_IN
    o2 = o1 + ML_IN
    o3 = o2 + HG_IN
    w_ssd = jnp.pad(w[:, :o1], ((0, 0), (0, SSD_INP - SSD_IN)))
    ml = w[:, o1:o2]
    w_ml = jnp.concatenate([ml[:, :ML_GATE_OFF], jnp.pad(ml[:, ML_GATE_OFF:ML_GATE_OFF + 2 * ML_HEADS],
                                                        ((0, 0), (0, LANES - 2 * ML_HEADS))),
                            ml[:, ML_GATE_OFF + 2 * ML_HEADS:]], axis=1)
    return tuple(x.astype(BF16) for x in (w_ssd, w_ml, w[:, o2:o3], w[:, o3:]))


def kernel(x_prompt, x_sample, state_ssd_conv, state_ssd, state_mlstm_c, state_mlstm_n, state_mlstm_m, state_hgrn, state_rwkv_shift, state_rwkv, w_in, ssd_conv_w, ssd_conv_b, ssd_dt_bias, ssd_a_log, ssd_d, ssd_norm_w, ml_b_if, ml_norm_w, hg_lower_bounds, hg_norm_w, rw_mu, rw_w0, rw_w2, rw_a0, rw_a2, rw_g2, rw_k_k, rw_k_a, rw_r_k, rw_ln_w, rw_ln_b, w_out, ln1_g, ln1_b, router_w, router_b, exp_w_gu, exp_b_gu, exp_w_down, exp_b_down, ln2_g, ln2_b):
    bp, tp, d = x_prompt.shape
    bs, ts, _ = x_sample.shape
    n_p = bp * tp
    depth = w_in.shape[0]
    alpha = float((2 * depth) ** 0.25)
    caches = (state_ssd_conv, state_ssd, state_mlstm_c, state_mlstm_n, state_mlstm_m,
              state_hgrn, state_rwkv_shift, state_rwkv)
    lbs = jax.nn.softmax(hg_lower_bounds.astype(F32), axis=0)
    lbs = jnp.cumsum(lbs, axis=0) - lbs[0]
    h = jnp.concatenate([x_prompt.reshape(n_p, d), x_sample.reshape(bs * ts, d)], axis=0)
    hb = h.astype(BF16)
    new_p, new_s = [], []
    for l in range(depth):
        p = dict(
            conv_w=ssd_conv_w[l], conv_b=ssd_conv_b[l].reshape(1, -1), dt_bias=_pad_lanes(ssd_dt_bias[l]),
            a_log=_pad_lanes(ssd_a_log[l]), d_skip=_pad_lanes(ssd_d[l]), ssd_nw=ssd_norm_w[l].reshape(1, -1),
            ml_bif=_pad_lanes(ml_b_if[l]), ml_nw=ml_norm_w[l].reshape(1, -1),
            hg_lb=lbs[l].reshape(1, -1), hg_nw=hg_norm_w[l].reshape(1, -1),
            rw_mu=rw_mu[l].reshape(1, -1), rw_w0=rw_w0[l].reshape(1, -1),
            rw_w2p=jnp.pad(rw_w2[l], ((0, RW_LR_A), (0, 0))), rw_a0=rw_a0[l].reshape(1, -1),
            rw_a2p=jnp.pad(rw_a2[l], ((RW_LR_W, 0), (0, 0))), rw_g2=rw_g2[l],
            rw_kk=rw_k_k[l].reshape(1, -1), rw_ka=rw_k_a[l].reshape(1, -1), rw_rk=rw_r_k[l],
            rw_lnw=rw_ln_w[l].reshape(RW_HEADS, RW_HEADDIM), rw_lnb=rw_ln_b[l].reshape(RW_HEADS, RW_HEADDIM),
        )
        us = tuple(_matmul(hb, w) for w in _split_w_in(w_in[l]))
        st_p = tuple(jnp.zeros((bp,) + c.shape[2:], F32) for c in caches)
        st_s = tuple(c[l] for c in caches)
        yp, sp = _token_mixers(us, 0, bp, tp, st_p, p)
        ys, ss = _token_mixers(us, n_p, bs, ts, st_s, p)
        mix = _matmul(jnp.concatenate([yp, ys], axis=0), w_out[l].astype(BF16))
        h, hb = _moe_layer(alpha, h, mix, ln1_g[l], ln1_b[l], router_w[l], router_b[l], exp_w_gu[l], exp_b_gu[l],
                           exp_w_down[l], exp_b_down[l], ln2_g[l], ln2_b[l])
        new_p.append(sp)
        new_s.append(ss)
    outs_p = [jnp.stack([s[i] for s in new_p]) for i in range(8)]
    outs_s = [jnp.stack([s[i] for s in new_s]) for i in range(8)]
    y_prompt = h[:n_p].reshape(bp, tp, d)
    y_sample = h[n_p:].reshape(bs, ts, d)
    res = [y_prompt, y_sample]
    for a, b_ in zip(outs_p, outs_s):
        res += [a, b_]
    return tuple(res)
```

```python
import functools

import jax
import jax.numpy as jnp
from jax import lax
from jax.experimental import pallas as pl
from jax.experimental.pallas import tpu as pltpu

F32 = jnp.float32
BF16 = jnp.bfloat16
HI = lax.Precision.HIGHEST
NT_DIMS = (((1,), (1,)), ((), ()))
TN_DIMS = (((0,), (0,)), ((), ()))

GROUP_W = 1024
CHUNK = 64
SSD_HEADDIM, SSD_HEADS, SSD_STATE, SSD_GROUPS, SSD_CONV = 64, 16, 128, 2, 4
SSD_CONV_CH = GROUP_W + 2 * SSD_GROUPS * SSD_STATE
SSD_IN = GROUP_W + SSD_CONV_CH + SSD_HEADS
SSD_INP = 2816
ML_HEADS, ML_DV, ML_DK = 4, 256, 128
ML_IN = 2 * ML_HEADS * ML_DK + GROUP_W + 2 * ML_HEADS + GROUP_W
ML_INP = 3328
HG_HEADS, HG_DK, HG_DV = 8, 128, 128
HG_IN = 4 * GROUP_W
RW_HEADS, RW_HEADDIM = 16, 64
RW_LR_W, RW_LR_A, RW_LR_G = 64, 64, 128
RW_IN = 3 * GROUP_W + RW_LR_W + RW_LR_A + RW_LR_G
RW_LN_EPS = 64e-5
TOP_K = 4
SWIGLU_LIMIT = 7.0
SWIGLU_ALPHA = 1.702
LN_EPS = 1e-5
RMS_EPS = 1e-6
NEG_BIG = -1e30
LANES = 128
MOE_TM = 256
VMEM_LIMIT = 56 * 1024 * 1024


def _cparams(n_axes, vmem=VMEM_LIMIT):
    return pltpu.CompilerParams(dimension_semantics=("arbitrary",) * n_axes, vmem_limit_bytes=vmem)


def _mm(a, b):
    return jnp.dot(a.astype(BF16), b.astype(BF16), preferred_element_type=F32)


def _mm_nt(a, b):
    return lax.dot_general(a.astype(BF16), b.astype(BF16), NT_DIMS, preferred_element_type=F32)


def _mm_tn(a, b):
    return lax.dot_general(a.astype(BF16), b.astype(BF16), TN_DIMS, preferred_element_type=F32)


def _mm_hi(a, b):
    return jnp.dot(a, b, precision=HI, preferred_element_type=F32)


def _tril(n, strict=False):
    r = lax.broadcasted_iota(jnp.int32, (n, n), 0)
    c = lax.broadcasted_iota(jnp.int32, (n, n), 1)
    return (r > c) if strict else (r >= c)


def _cumsum_rows(x):
    return _mm_hi(_tril(x.shape[0]).astype(F32), x)


def _transpose_lanes(x):
    n = x.shape[1]
    eye = (lax.broadcasted_iota(jnp.int32, (n, n), 0) == lax.broadcasted_iota(jnp.int32, (n, n), 1)).astype(F32)
    return lax.dot_general(eye, x, NT_DIMS, precision=HI, preferred_element_type=F32)


def _sigmoid(x):
    return 1.0 / (1.0 + jnp.exp(-x))


def _silu(x):
    return x * _sigmoid(x)


def _softplus(x):
    return jnp.maximum(x, 0.0) + jnp.log1p(jnp.exp(-jnp.abs(x)))


def _layer_norm_rows(x, g, b, eps):
    mu = jnp.mean(x, axis=-1, keepdims=True)
    xc = x - mu
    var = jnp.mean(xc * xc, axis=-1, keepdims=True)
    return xc * lax.rsqrt(var + eps) * g + b


def _matmul_kernel(x_ref, w_ref, o_ref):
    o_ref[...] = jnp.dot(x_ref[...], w_ref[...], preferred_element_type=F32).astype(o_ref.dtype)


def _pick_tile(n, cands):
    for c in cands:
        if n % c == 0:
            return c
    return n


def _matmul(x, w, out_dtype=F32):
    m, k = x.shape
    n = w.shape[1]
    tm = _pick_tile(m, (1088, 1024, 512, 256, 128, 64, 32, 16, 8))
    tn = _pick_tile(n, (512, 256, 128))
    return pl.pallas_call(
        _matmul_kernel,
        grid=(m // tm, n // tn),
        in_specs=[pl.BlockSpec((tm, k), lambda i, j: (i, 0)), pl.BlockSpec((k, tn), lambda i, j: (0, j))],
        out_specs=pl.BlockSpec((tm, tn), lambda i, j: (i, j)),
        out_shape=jax.ShapeDtypeStruct((m, n), out_dtype),
        compiler_params=_cparams(2),
        name="dense_matmul",
    )(x, w)


def _ssd_kernel(L, u_ref, conv0_ref, s0_ref, cw_ref, cb_ref, dtb_ref, alog_ref, dsk_ref, nw_ref,
                y_ref, convo_ref, so_ref, xp_scr, s_scr, y_scr):
    c = pl.program_id(1)

    @pl.when(c == 0)
    def _():
        xp_scr[5:8, :] = conv0_ref[...]
        s_scr[...] = s0_ref[...]

    u = u_ref[...]
    xp_scr[8:8 + L, :] = u[:, GROUP_W:GROUP_W + SSD_CONV_CH]
    conv = xp_scr[5:5 + L, :] * cw_ref[0:1, :]
    for j in range(1, SSD_CONV):
        conv = conv + xp_scr[5 + j:5 + j + L, :] * cw_ref[j:j + 1, :]
    conv = cb_ref[...] + conv
    tail = xp_scr[5 + L:8 + L, :]
    xp_scr[5:8, :] = tail
    convo_ref[...] = tail

    xbc = _silu(conv)
    xh = xbc[:, :GROUP_W]
    bm = xbc[:, GROUP_W:GROUP_W + SSD_GROUPS * SSD_STATE]
    cm = xbc[:, GROUP_W + SSD_GROUPS * SSD_STATE:]
    dt_off = GROUP_W + SSD_CONV_CH
    dtp = _softplus(u[:, dt_off:dt_off + LANES] + dtb_ref[...])
    lap = dtp * (-jnp.exp(alog_ref[...]))
    cum = _cumsum_rows(lap)
    cum_t = _transpose_lanes(cum)
    dt_t = _transpose_lanes(dtp)
    cumlast = cum[L - 1:L, :]
    mask = _tril(L)
    rep = SSD_HEADS // SSD_GROUPS
    for g in range(SSD_GROUPS):
        bg = bm[:, g * SSD_STATE:(g + 1) * SSD_STATE]
        cg = cm[:, g * SSD_STATE:(g + 1) * SSD_STATE]
        cb = _mm_nt(cg, bg)
        for r8 in range(rep):
            r = g * rep + r8
            col = cum[:, r:r + 1]
            row = cum_t[r:r + 1, :]
            dec = jnp.where(mask, jnp.exp(jnp.where(mask, col - row, 0.0)), 0.0)
            w = cb * dec * dt_t[r:r + 1, :]
            xr = xh[:, r * SSD_HEADDIM:(r + 1) * SSD_HEADDIM]
            sr = s_scr[r]
            y = _mm(w, xr) + _mm_nt(cg, sr) * jnp.exp(col)
            cl = cumlast[:, r:r + 1]
            wend = jnp.exp(cl - col) * dtp[:, r:r + 1]
            s_scr[r] = sr * jnp.exp(cl) + _mm_tn(wend * xr, bg)
            y = y + dsk_ref[:, r:r + 1] * xr
            y_scr[:, r * SSD_HEADDIM:(r + 1) * SSD_HEADDIM] = y

    yz = y_scr[...] * _silu(u[:, :GROUP_W])
    gw = GROUP_W // SSD_GROUPS
    for g in range(SSD_GROUPS):
        yg = yz[:, g * gw:(g + 1) * gw]
        ms = jnp.mean(yg * yg, axis=-1, keepdims=True)
        y_ref[:, g * gw:(g + 1) * gw] = (yg * lax.rsqrt(ms + RMS_EPS) * nw_ref[:, g * gw:(g + 1) * gw]).astype(y_ref.dtype)

    @pl.when(c == pl.num_programs(1) - 1)
    def _():
        so_ref[...] = s_scr[...]


def _ssd_mixer(u2d, row0, b, t, conv0, s0, cw, cb, dtb, alog, dsk, nw):
    L = min(t, CHUNK)
    nc = t // L
    blk0 = row0 // L
    const = lambda shape: pl.BlockSpec(shape, lambda i, c: (0,) * len(shape))
    return pl.pallas_call(
        functools.partial(_ssd_kernel, L),
        grid=(b, nc),
        in_specs=[
            pl.BlockSpec((L, SSD_INP), lambda i, c: (blk0 + i * nc + c, 0)),
            pl.BlockSpec((None, SSD_CONV - 1, SSD_CONV_CH), lambda i, c: (i, 0, 0)),
            pl.BlockSpec((None, SSD_HEADS, SSD_HEADDIM, SSD_STATE), lambda i, c: (i, 0, 0, 0)),
            const((SSD_CONV, SSD_CONV_CH)), const((1, SSD_CONV_CH)), const((1, LANES)), const((1, LANES)),
            const((1, LANES)), const((1, GROUP_W)),
        ],
        out_specs=[
            pl.BlockSpec((L, GROUP_W), lambda i, c: (i * nc + c, 0)),
            pl.BlockSpec((None, SSD_CONV - 1, SSD_CONV_CH), lambda i, c: (i, 0, 0)),
            pl.BlockSpec((None, SSD_HEADS, SSD_HEADDIM, SSD_STATE), lambda i, c: (i, 0, 0, 0)),
        ],
        out_shape=[
            jax.ShapeDtypeStruct((b * t, GROUP_W), BF16),
            jax.ShapeDtypeStruct((b, SSD_CONV - 1, SSD_CONV_CH), F32),
            jax.ShapeDtypeStruct((b, SSD_HEADS, SSD_HEADDIM, SSD_STATE), F32),
        ],
        scratch_shapes=[pltpu.VMEM((L + 8, SSD_CONV_CH), F32), pltpu.VMEM((SSD_HEADS, SSD_HEADDIM, SSD_STATE), F32),
                        pltpu.VMEM((L, GROUP_W), F32)],
        compiler_params=_cparams(2),
        name="ssd_mixer",
    )(u2d, conv0, s0, cw, cb, dtb, alog, dsk, nw)


ML_QK = ML_HEADS * ML_DK
ML_GATE_OFF = 2 * ML_QK + GROUP_W
ML_O_OFF = ML_GATE_OFF + LANES


def _mlstm_kernel(L, u_ref, c0_ref, n0_ref, m0_ref, bif_ref, nw_ref, y_ref, co_ref, no_ref, mo_ref,
                  c_scr, n_scr, m_scr):
    c = pl.program_id(1)

    @pl.when(c == 0)
    def _():
        c_scr[...] = c0_ref[...]
        n_scr[...] = n0_ref[...]
        m_scr[...] = m0_ref[...]

    u = u_ref[...]
    gates = u[:, ML_GATE_OFF:ML_GATE_OFF + LANES] + bif_ref[...]
    lf = -_softplus(-gates)
    bcum = _cumsum_rows(lf)
    b_t = _transpose_lanes(bcum)
    g_t = _transpose_lanes(gates)
    mrow = m_scr[...]
    mask = _tril(L)
    for h in range(ML_HEADS):
        q = u[:, h * ML_DK:(h + 1) * ML_DK] * (ML_DK ** -0.5)
        k = u[:, ML_QK + h * ML_DK:ML_QK + (h + 1) * ML_DK]
        v = u[:, 2 * ML_QK + h * ML_DV:2 * ML_QK + (h + 1) * ML_DV]
        fh = ML_HEADS + h
        bcol = bcum[:, fh:fh + 1]
        brow = b_t[fh:fh + 1, :]
        irow = g_t[h:h + 1, :]
        icol = gates[:, h:h + 1]
        m_h = mrow[:, h:h + 1]
        d = jnp.where(mask, bcol - brow + irow, NEG_BIG)
        gg = bcol + m_h
        mt = jnp.maximum(gg, jnp.max(d, axis=1, keepdims=True))
        wts = jnp.where(mask, jnp.exp(d - mt), 0.0)
        inter = jnp.exp(gg - mt)
        qk = _mm_nt(q, k) * wts
        c_h = c_scr[h]
        n_h = n_scr[h:h + 1, :]
        num = _mm(qk, v) + inter * _mm_nt(q, c_h)
        den = jnp.sum(qk, axis=1, keepdims=True) + inter * jnp.sum(q * n_h, axis=1, keepdims=True)
        hh = num / jnp.maximum(jnp.abs(den), jnp.exp(-mt))
        bl = bcum[L - 1:L, fh:fh + 1]
        loc = bl - bcol + icol
        m_new = jnp.maximum(bl + m_h, jnp.max(loc, axis=0, keepdims=True))
        wl = jnp.exp(loc - m_new)
        decay = jnp.exp(bl + m_h - m_new)
        c_scr[h] = decay * c_h + _mm_tn(wl * v, k)
        n_scr[h:h + 1, :] = decay * n_h + jnp.sum(wl * k, axis=0, keepdims=True)
        m_scr[:, h:h + 1] = m_new
        ms = jnp.mean(hh * hh, axis=-1, keepdims=True)
        hn = hh * lax.rsqrt(ms + RMS_EPS) * nw_ref[:, h * ML_DV:(h + 1) * ML_DV]
        o = u[:, ML_O_OFF + h * ML_DV:ML_O_OFF + (h + 1) * ML_DV]
        y_ref[:, h * ML_DV:(h + 1) * ML_DV] = (hn * _sigmoid(o)).astype(y_ref.dtype)

    @pl.when(c == pl.num_programs(1) - 1)
    def _():
        co_ref[...] = c_scr[...]
        no_ref[...] = n_scr[...]
        mo_ref[...] = m_scr[...]


def _mlstm_mixer(u2d, row0, b, t, c0, n0, m0p, bif, nw):
    L = min(t, CHUNK)
    nc = t // L
    blk0 = row0 // L
    const = lambda shape: pl.BlockSpec(shape, lambda i, c: (0,) * len(shape))
    return pl.pallas_call(
        functools.partial(_mlstm_kernel, L),
        grid=(b, nc),
        in_specs=[
            pl.BlockSpec((L, ML_INP), lambda i, c: (blk0 + i * nc + c, 0)),
            pl.BlockSpec((None, ML_HEADS, ML_DV, ML_DK), lambda i, c: (i, 0, 0, 0)),
            pl.BlockSpec((None, ML_HEADS, ML_DK), lambda i, c: (i, 0, 0)),
            pl.BlockSpec((None, 1, LANES), lambda i, c: (i, 0, 0)),
            const((1, LANES)), const((1, GROUP_W)),
        ],
        out_specs=[
            pl.BlockSpec((L, GROUP_W), lambda i, c: (i * nc + c, 0)),
            pl.BlockSpec((None, ML_HEADS, ML_DV, ML_DK), lambda i, c: (i, 0, 0, 0)),
            pl.BlockSpec((None, ML_HEADS, ML_DK), lambda i, c: (i, 0, 0)),
            pl.BlockSpec((None, 1, LANES), lambda i, c: (i, 0, 0)),
        ],
        out_shape=[
            jax.ShapeDtypeStruct((b * t, GROUP_W), BF16),
            jax.ShapeDtypeStruct((b, ML_HEADS, ML_DV, ML_DK), F32),
            jax.ShapeDtypeStruct((b, ML_HEADS, ML_DK), F32),
            jax.ShapeDtypeStruct((b, 1, LANES), F32),
        ],
        scratch_shapes=[pltpu.VMEM((ML_HEADS, ML_DV, ML_DK), F32), pltpu.VMEM((ML_HEADS, ML_DK), F32),
                        pltpu.VMEM((1, LANES), F32)],
        compiler_params=_cparams(2),
        name="mlstm_mixer",
    )(u2d, c0, n0, m0p, bif, nw)


def _gla_kernel(L, u_ref, s0_ref, lb_ref, nw_ref, y_ref, so_ref, s_scr, g_scr, k_scr, q_scr, a_scr):
    c = pl.program_id(1)

    @pl.when(c == 0)
    def _():
        s_scr[...] = s0_ref[...]

    u = u_ref[...]
    lb = lb_ref[...]
    q = _silu(u[:, :GROUP_W])
    fp = u[:, GROUP_W:2 * GROUP_W]
    f = lb + (1.0 - lb) * _sigmoid(fp)
    k = (1.0 - lb) * _sigmoid(-fp)
    v = u[:, 2 * GROUP_W:3 * GROUP_W]
    gate = u[:, 3 * GROUP_W:]
    gc = _cumsum_rows(jnp.log(f))
    g_scr[...] = gc
    k_scr[...] = k
    q_scr[...] = q
    for h in range(HG_HEADS):
        sl = slice(h * HG_DK, (h + 1) * HG_DK)
        gh, qh, kh, vh = gc[:, sl], q[:, sl], k[:, sl], v[:, sl]

        a_scr[...] = jnp.zeros((L, L), F32)
        for band in range(L // 16):
            r0 = band * 16
            gh_b = g_scr[r0:, sl]
            qh_b = q_scr[r0:, sl]
            rows_b = lax.broadcasted_iota(jnp.int32, (L - r0, 1), 0) + r0
            lanes_b = lax.broadcasted_iota(jnp.int32, (L - r0, L), 1)

            def body(sb, carry, gh_b=gh_b, qh_b=qh_b, rows_b=rows_b, lanes_b=lanes_b, sl=sl, r0=r0):
                base = pl.multiple_of(sb * 8, 8)
                g8 = g_scr[pl.ds(base, 8), sl]
                k8 = k_scr[pl.ds(base, 8), sl]
                a_low = a_scr[r0:, :]
                for j in range(8):
                    s = base + j
                    m = rows_b >= s
                    e = jnp.where(m, jnp.exp(jnp.where(m, gh_b - g8[j:j + 1, :], 0.0)), 0.0)
                    a = jnp.sum(qh_b * e * k8[j:j + 1, :], axis=1, keepdims=True)
                    a_low = jnp.where(lanes_b == s, a, a_low)
                a_scr[r0:, :] = a_low
                return carry

            lax.fori_loop(2 * band, 2 * band + 2, body, 0)
        a_mat = a_scr[...]
        st = s_scr[h]
        o = _mm(a_mat, vh) + _mm_nt(qh * jnp.exp(gh), st)
        gl = gc[L - 1:L, sl]
        s_scr[h] = st * jnp.exp(gl) + _mm_tn(vh, kh * jnp.exp(gl - gh))
        ms = jnp.mean(o * o, axis=-1, keepdims=True)
        on = o * lax.rsqrt(ms + RMS_EPS) * nw_ref[:, sl]
        y_ref[:, sl] = (on * _silu(gate[:, sl])).astype(y_ref.dtype)

    @pl.when(c == pl.num_programs(1) - 1)
    def _():
        so_ref[...] = s_scr[...]


def _gla_mixer(u2d, row0, b, t, s0t, lb, nw):
    L = min(t, CHUNK)
    nc = t // L
    blk0 = row0 // L
    const = lambda shape: pl.BlockSpec(shape, lambda i, c: (0,) * len(shape))
    return pl.pallas_call(
        functools.partial(_gla_kernel, L),
        grid=(b, nc),
        in_specs=[
            pl.BlockSpec((L, HG_IN), lambda i, c: (blk0 + i * nc + c, 0)),
            pl.BlockSpec((None, HG_HEADS, HG_DV, HG_DK), lambda i, c: (i, 0, 0, 0)),
            const((1, GROUP_W)), const((1, GROUP_W)),
        ],
        out_specs=[
            pl.BlockSpec((L, GROUP_W), lambda i, c: (i * nc + c, 0)),
            pl.BlockSpec((None, HG_HEADS, HG_DV, HG_DK), lambda i, c: (i, 0, 0, 0)),
        ],
        out_shape=[
            jax.ShapeDtypeStruct((b * t, GROUP_W), BF16),
            jax.ShapeDtypeStruct((b, HG_HEADS, HG_DV, HG_DK), F32),
        ],
        scratch_shapes=[pltpu.VMEM((HG_HEADS, HG_DV, HG_DK), F32), pltpu.VMEM((L, GROUP_W), F32),
                        pltpu.VMEM((L, GROUP_W), F32), pltpu.VMEM((L, GROUP_W), F32), pltpu.VMEM((L, L), F32)],
        compiler_params=_cparams(2),
        name="gla_mixer",
    )(u2d, s0t, lb, nw)


RW_LORA_OFF = 3 * GROUP_W


def _rw_prep_kernel(tr, u_ref, sh0_ref, mu_ref, w0_ref, w2p_ref, a0_ref, a2p_ref, g2_ref, kk_ref, ka_ref,
                    r_o, w_o, k_o, v_o, kk_o, a_o, g_o, sh_o, buf):
    c = pl.program_id(1)

    @pl.when(c == 0)
    def _():
        buf[7:8, :] = sh0_ref[...]

    u = u_ref[...]
    buf[8:8 + tr, :] = u
    prev = buf[7:7 + tr, :]
    last = u[tr - 1:tr, :]
    buf[7:8, :] = last
    sh_o[...] = last
    um = u + (prev - u) * mu_ref[...]
    r = um[:, :GROUP_W]
    k = um[:, GROUP_W:2 * GROUP_W]
    v = um[:, 2 * GROUP_W:3 * GROUP_W]
    xwa = um[:, RW_LORA_OFF:RW_LORA_OFF + LANES]
    xg = um[:, RW_LORA_OFF + LANES:]
    w_log = -_softplus(-(w0_ref[...] + _mm(jnp.tanh(xwa), w2p_ref[...]))) - 0.5
    a = _sigmoid(a0_ref[...] + _mm(xwa, a2p_ref[...]))
    r_o[...] = r
    w_o[...] = jnp.exp(-jnp.exp(w_log))
    k_o[...] = k * (1.0 + (a - 1.0) * ka_ref[...])
    v_o[...] = v
    kk_o[...] = k * kk_ref[...]
    a_o[...] = a
    g_o[...] = _mm(_sigmoid(xg), g2_ref[...])


def _rw_prep(u2d, row0, b, t, sh0, mu, w0, w2p, a0, a2p, g2, kk, ka):
    tr = _pick_tile(t, (256, 128, 64, 32, 16, 8))
    nc = t // tr
    blk0 = row0 // tr
    const = lambda shape: pl.BlockSpec(shape, lambda i, c: (0,) * len(shape))
    row_out = pl.BlockSpec((tr, GROUP_W), lambda i, c: (i * nc + c, 0))
    return pl.pallas_call(
        functools.partial(_rw_prep_kernel, tr),
        grid=(b, nc),
        in_specs=[
            pl.BlockSpec((tr, RW_IN), lambda i, c: (blk0 + i * nc + c, 0)),
            pl.BlockSpec((None, 1, RW_IN), lambda i, c: (i, 0, 0)),
            const((1, RW_IN)), const((1, GROUP_W)), const((LANES, GROUP_W)), const((1, GROUP_W)),
            const((LANES, GROUP_W)), const((RW_LR_G, GROUP_W)), const((1, GROUP_W)), const((1, GROUP_W)),
        ],
        out_specs=[row_out] * 7 + [pl.BlockSpec((None, 1, RW_IN), lambda i, c: (i, 0, 0))],
        out_shape=[jax.ShapeDtypeStruct((b * t, GROUP_W), F32)] * 7 + [jax.ShapeDtypeStruct((b, 1, RW_IN), F32)],
        scratch_shapes=[pltpu.VMEM((tr + 8, RW_IN), F32)],
        compiler_params=_cparams(2),
        name="rwkv_prep",
    )(u2d, sh0, mu, w0, w2p, a0, a2p, g2, kk, ka)


def _rw_scan_kernel(tc, r_ref, w_ref, k_ref, v_ref, kk_ref, a_ref, g_ref, s0_ref, rk_ref, lnw_ref, lnb_ref,
                    y_ref, so_ref, s_scr, nkk_scr, kka_scr, yraw_scr):
    c = pl.program_id(1)

    @pl.when(c == 0)
    def _():
        s_scr[...] = s0_ref[...]

    kk = kk_ref[...]
    kk = kk / jnp.maximum(jnp.sqrt(jnp.sum(kk * kk, axis=-1, keepdims=True)), 1e-12)
    nkk_scr[...] = -kk
    kka_scr[...] = kk * a_ref[...]

    shp = (RW_HEADDIM, RW_HEADS, RW_HEADDIM)
    eye = lax.broadcasted_iota(jnp.int32, shp, 0) == lax.broadcasted_iota(jnp.int32, shp, 2)

    def two_steps(i, carry):
        s = s_scr[...]
        for j in range(2):
            t = 2 * i + j
            sa = jnp.sum(s * nkk_scr[t][None], axis=-1, keepdims=True)
            vcol = jnp.sum(jnp.where(eye, v_ref[t][None], 0.0), axis=-1, keepdims=True)
            s = s * w_ref[t][None] + sa * kka_scr[t][None] + vcol * k_ref[t][None]
            ycol = jnp.sum(s * r_ref[t][None], axis=-1, keepdims=True)
            yraw_scr[t] = jnp.sum(jnp.where(eye, ycol, 0.0), axis=0)
        s_scr[...] = s
        return carry

    lax.fori_loop(0, tc // 2, two_steps, 0)

    r, k, v = r_ref[...], k_ref[...], v_ref[...]
    y = _layer_norm_rows(yraw_scr[...], lnw_ref[...], lnb_ref[...], RW_LN_EPS)
    y = y + jnp.sum(r * k * rk_ref[...], axis=-1, keepdims=True) * v
    y_ref[...] = y * g_ref[...]

    @pl.when(c == pl.num_programs(1) - 1)
    def _():
        so_ref[...] = s_scr[...]


def _rw_scan(seqs, b, t, s0t, rk, lnw, lnb):
    tc = _pick_tile(t, (128, 64, 32, 16, 8))
    nc = t // tc
    seq_spec = pl.BlockSpec((None, tc, RW_HEADS, RW_HEADDIM), lambda i, c: (i, c, 0, 0))
    st_spec = pl.BlockSpec((None, RW_HEADDIM, RW_HEADS, RW_HEADDIM), lambda i, c: (i, 0, 0, 0))
    par_spec = pl.BlockSpec((RW_HEADS, RW_HEADDIM), lambda i, c: (0, 0))
    seqs4 = [x.reshape(b, t, RW_HEADS, RW_HEADDIM) for x in seqs]
    return pl.pallas_call(
        functools.partial(_rw_scan_kernel, tc),
        grid=(b, nc),
        in_specs=[seq_spec] * 7 + [st_spec, par_spec, par_spec, par_spec],
        out_specs=[seq_spec, st_spec],
        out_shape=[jax.ShapeDtypeStruct((b, t, RW_HEADS, RW_HEADDIM), F32),
                   jax.ShapeDtypeStruct((b, RW_HEADDIM, RW_HEADS, RW_HEADDIM), F32)],
        scratch_shapes=[pltpu.VMEM((RW_HEADDIM, RW_HEADS, RW_HEADDIM), F32)]
        + [pltpu.VMEM((tc, RW_HEADS, RW_HEADDIM), F32)] * 3,
        compiler_params=_cparams(2),
        name="rwkv_scan",
    )(*seqs4, s0t, rk, lnw, lnb)


def _ln_router_kernel(alpha, h_ref, mix_ref, g_ref, b_ref, rw_ref, rb_ref,
                      h1_ref, idx_ref, gate_ref, rank_ref, cnt_ref, cnt_scr):
    i = pl.program_id(0)

    @pl.when(i == 0)
    def _():
        cnt_scr[...] = jnp.zeros_like(cnt_scr)

    h1 = _layer_norm_rows(alpha * h_ref[...] + mix_ref[...], g_ref[...], b_ref[...], LN_EPS)
    h1_ref[...] = h1
    tm = h1.shape[0]
    logits = _mm_hi(h1, rw_ref[...]) + rb_ref[...]
    lane = lax.broadcasted_iota(jnp.int32, (tm, LANES), 1)
    lg = logits
    vals, sels = [], []
    idx_out = jnp.zeros((tm, LANES), jnp.int32)
    for k in range(TOP_K):
        m = jnp.max(lg, axis=-1, keepdims=True)
        idx = jnp.min(jnp.where(lg == m, lane, LANES), axis=-1, keepdims=True)
        sel = lane == idx
        lg = jnp.where(sel, -3e38, lg)
        vals.append(m)
        sels.append(sel)
        idx_out = jnp.where(lane == k, idx, idx_out)
    es = [jnp.exp(v - vals[0]) for v in vals]
    den = es[0] + es[1] + es[2] + es[3]
    gate_out = jnp.zeros((tm, LANES), F32)
    multi = jnp.zeros((tm, LANES), F32)
    for k in range(TOP_K):
        gate_out = jnp.where(lane == k, es[k] / den, gate_out)
        multi = multi + sels[k].astype(F32)
    before = _mm(_tril(tm, strict=True).astype(F32), multi) + cnt_scr[...]
    rank_out = jnp.zeros((tm, LANES), jnp.int32)
    for k in range(TOP_K):
        rk = jnp.sum(jnp.where(sels[k], before, 0.0), axis=-1, keepdims=True)
        rank_out = jnp.where(lane == k, rk.astype(jnp.int32), rank_out)
    idx_ref[...] = idx_out
    gate_ref[...] = gate_out
    rank_ref[...] = rank_out
    cnt_scr[...] = cnt_scr[...] + jnp.sum(multi, axis=0, keepdims=True)
    cnt_ref[...] = cnt_scr[...]


def _ln_router(alpha, h, mix, g, b, rwp, rbp):
    n, d = h.shape
    tm = _pick_tile(n, (256, 128, 64, 32, 16, 8))
    row = lambda w: pl.BlockSpec((tm, w), lambda i: (i, 0))
    const = lambda shape: pl.BlockSpec(shape, lambda i: (0,) * len(shape))
    return pl.pallas_call(
        functools.partial(_ln_router_kernel, alpha),
        grid=(n // tm,),
        in_specs=[row(d), row(d), const((1, d)), const((1, d)), const((d, LANES)), const((1, LANES))],
        out_specs=[row(d), row(LANES), row(LANES), row(LANES), const((1, LANES))],
        out_shape=[jax.ShapeDtypeStruct((n, d), F32),
                   jax.ShapeDtypeStruct((n, LANES), jnp.int32), jax.ShapeDtypeStruct((n, LANES), F32),
                   jax.ShapeDtypeStruct((n, LANES), jnp.int32), jax.ShapeDtypeStruct((1, LANES), F32)],
        scratch_shapes=[pltpu.VMEM((1, LANES), F32)],
        compiler_params=_cparams(1),
        name="ln_router",
    )(h, mix, g, b, rwp, rbp)


def _dispatch_kernel(td, pos_ref, h_ref, xs_in_ref, xs_ref, sem):
    del xs_in_ref

    def row_copy(r, p):
        return pltpu.make_async_copy(h_ref.at[pl.ds(r, 1)], xs_ref.at[pl.ds(p, 1)], sem)

    def issue(r, carry):
        for k in range(TOP_K):
            row_copy(r, pos_ref[0, r * TOP_K + k]).start()
        return carry

    def drain(r, carry):
        for k in range(TOP_K):
            row_copy(r, pos_ref[0, r * TOP_K + k]).wait()
        return carry

    lax.fori_loop(0, td, issue, 0)
    lax.fori_loop(0, td, drain, 0)


def _dispatch(h1, pos, p_rows):
    n, d = h1.shape
    td = _pick_tile(n, (128, 64, 32, 16, 8))
    pos2 = pos.reshape(n // td, 1, td * TOP_K)
    xs0 = jnp.zeros((p_rows, d), F32)
    return pl.pallas_call(
        functools.partial(_dispatch_kernel, td),
        grid=(n // td,),
        in_specs=[pl.BlockSpec((None, 1, td * TOP_K), lambda i: (i, 0, 0), memory_space=pltpu.SMEM),
                  pl.BlockSpec((td, d), lambda i: (i, 0)),
                  pl.BlockSpec(memory_space=pl.ANY)],
        out_specs=pl.BlockSpec(memory_space=pl.ANY),
        out_shape=jax.ShapeDtypeStruct((p_rows, d), F32),
        scratch_shapes=[pltpu.SemaphoreType.DMA],
        input_output_aliases={2: 0},
        compiler_params=_cparams(1),
        name="moe_dispatch",
    )(pos2, h1, xs0)


def _gemm1_kernel(se, sn, so, sm, sf, sv, x_ref, wg_ref, wu_ref, bg_ref, bu_ref, o_ref, wgb, wub):
    s = pl.program_id(0)

    @pl.when(sf[s] == 1)
    def _():
        wgb[...] = wg_ref[...].astype(BF16)
        wub[...] = wu_ref[...].astype(BF16)

    @pl.when(sv[s] == 1)
    def _():
        x = x_ref[...]
        hg = jnp.dot(x, wgb[...], preferred_element_type=F32) + bg_ref[...]
        hu = jnp.dot(x, wub[...], preferred_element_type=F32) + bu_ref[...]
        hg = jnp.minimum(hg, SWIGLU_LIMIT)
        hu = jnp.clip(hu, -SWIGLU_LIMIT, SWIGLU_LIMIT)
        o_ref[...] = (hg * _sigmoid(SWIGLU_ALPHA * hg) * (hu + 1.0)).astype(o_ref.dtype)

    @pl.when(sv[s] == 0)
    def _():
        o_ref[...] = jnp.zeros_like(o_ref)


def _gemm1(sched, xs, w_gu, b_gu, tn):
    p_rows, d = xs.shape
    n_exp, _, f2 = w_gu.shape
    ff = f2 // 2
    nb = ff // tn
    n_steps = sched[0].shape[0]
    b3 = b_gu.reshape(n_exp, 1, f2)
    grid_spec = pltpu.PrefetchScalarGridSpec(
        num_scalar_prefetch=6,
        grid=(n_steps,),
        in_specs=[
            pl.BlockSpec((MOE_TM, d), lambda s, se, sn, so, sm, sf, sv: (sm[s], 0)),
            pl.BlockSpec((None, d, tn), lambda s, se, sn, so, sm, sf, sv: (se[s], 0, sn[s])),
            pl.BlockSpec((None, d, tn), lambda s, se, sn, so, sm, sf, sv: (se[s], 0, nb + sn[s])),
            pl.BlockSpec((None, 1, tn), lambda s, se, sn, so, sm, sf, sv: (se[s], 0, sn[s])),
            pl.BlockSpec((None, 1, tn), lambda s, se, sn, so, sm, sf, sv: (se[s], 0, nb + sn[s])),
        ],
        out_specs=pl.BlockSpec((MOE_TM, tn), lambda s, se, sn, so, sm, sf, sv: (sm[s], so[s])),
        scratch_shapes=[pltpu.VMEM((d, tn), BF16), pltpu.VMEM((d, tn), BF16)],
    )
    return pl.pallas_call(
        _gemm1_kernel,
        grid_spec=grid_spec,
        out_shape=jax.ShapeDtypeStruct((p_rows, ff), BF16),
        compiler_params=_cparams(1),
        name="moe_gate_up",
    )(*sched, xs, w_gu, w_gu, b3, b3)


def _gemm2_kernel(se, sn, so, sm, sf, sv, x_ref, w_ref, b_ref, o_ref, wb):
    s = pl.program_id(0)

    @pl.when(sf[s] == 1)
    def _():
        wb[...] = w_ref[...].astype(BF16)

    @pl.when(sv[s] == 1)
    def _():
        o_ref[...] = jnp.dot(x_ref[...], wb[...], preferred_element_type=F32) + b_ref[...]

    @pl.when(sv[s] == 0)
    def _():
        o_ref[...] = jnp.zeros_like(o_ref)


def _gemm2(sched, act, w_down, b_down, tn):
    p_rows, ff = act.shape
    n_exp, _, d = w_down.shape
    n_steps = sched[0].shape[0]
    b3 = b_down.reshape(n_exp, 1, d)
    grid_spec = pltpu.PrefetchScalarGridSpec(
        num_scalar_prefetch=6,
        grid=(n_steps,),
        in_specs=[
            pl.BlockSpec((MOE_TM, ff), lambda s, se, sn, so, sm, sf, sv: (sm[s], 0)),
            pl.BlockSpec((None, ff, tn), lambda s, se, sn, so, sm, sf, sv: (se[s], 0, sn[s])),
            pl.BlockSpec((None, 1, tn), lambda s, se, sn, so, sm, sf, sv: (se[s], 0, sn[s])),
        ],
        out_specs=pl.BlockSpec((MOE_TM, tn), lambda s, se, sn, so, sm, sf, sv: (sm[s], so[s])),
        scratch_shapes=[pltpu.VMEM((ff, tn), BF16)],
    )
    return pl.pallas_call(
        _gemm2_kernel,
        grid_spec=grid_spec,
        out_shape=jax.ShapeDtypeStruct((p_rows, d), F32),
        compiler_params=_cparams(1),
        name="moe_down",
    )(*sched, act, w_down, b3)


def _moe_schedule(counts, n_blocks, max_tiles, first_expert=0):
    n_exp = counts.shape[0]
    tiles = (counts + MOE_TM - 1) // MOE_TM
    tile_start = jnp.cumsum(tiles) - tiles
    steps_e = tiles * n_blocks
    step_end = jnp.cumsum(steps_e)
    step_start = step_end - steps_e
    total = step_end[-1]
    s = jnp.arange(max_tiles * n_blocks, dtype=jnp.int32)
    valid = s < total
    sc = jnp.minimum(s, jnp.maximum(total - 1, 0))
    e = jnp.minimum(jnp.sum((step_end[None, :] <= sc[:, None]).astype(jnp.int32), axis=1), n_exp - 1)
    local = sc - step_start[e]
    te = jnp.maximum(tiles[e], 1)
    j = local // te
    i = local % te
    first = jnp.logical_and(valid, i == 0)
    spare = jnp.maximum(s - total, 0)
    jo = jnp.where(valid, j, spare % n_blocks)
    mt = jnp.where(valid, tile_start[e] + i, jnp.sum(tiles) + spare // n_blocks)
    return ((e + first_expert).astype(jnp.int32), j.astype(jnp.int32), jo.astype(jnp.int32), mt.astype(jnp.int32),
            first.astype(jnp.int32), valid.astype(jnp.int32))


def _combine_kernel(alpha, tc, pos_ref, gate_ref, h1_ref, g_ref, b_ref, ys_ref, h2_ref, h2b_ref, buf, sem):
    def row_copy(r, k):
        return pltpu.make_async_copy(ys_ref.at[pl.ds(pos_ref[0, r * TOP_K + k], 1)], buf.at[k, pl.ds(r, 1)], sem)

    def issue(r, carry):
        for k in range(TOP_K):
            row_copy(r, k).start()
        return carry

    def drain(r, carry):
        for k in range(TOP_K):
            row_copy(r, k).wait()
        return carry

    lax.fori_loop(0, tc, issue, 0)
    lax.fori_loop(0, tc, drain, 0)
    gates = gate_ref[...]
    ffn = gates[:, 0:1] * buf[0]
    for k in range(1, TOP_K):
        ffn = ffn + gates[:, k:k + 1] * buf[k]
    h2 = _layer_norm_rows(alpha * h1_ref[...] + ffn, g_ref[...], b_ref[...], LN_EPS)
    h2_ref[...] = h2
    h2b_ref[...] = h2.astype(BF16)


def _combine(alpha, pos, gates, h1, g, b, ys):
    n, d = h1.shape
    tc = _pick_tile(n, (64, 32, 16, 8))
    pos2 = pos.reshape(n // tc, 1, tc * TOP_K)
    row = lambda w: pl.BlockSpec((tc, w), lambda i: (i, 0))
    const = lambda shape: pl.BlockSpec(shape, lambda i: (0,) * len(shape))
    return pl.pallas_call(
        functools.partial(_combine_kernel, alpha, tc),
        grid=(n // tc,),
        in_specs=[pl.BlockSpec((None, 1, tc * TOP_K), lambda i: (i, 0, 0), memory_space=pltpu.SMEM),
                  row(LANES), row(d), const((1, d)), const((1, d)), pl.BlockSpec(memory_space=pl.ANY)],
        out_specs=[row(d), row(d)],
        out_shape=[jax.ShapeDtypeStruct((n, d), F32), jax.ShapeDtypeStruct((n, d), BF16)],
        scratch_shapes=[pltpu.VMEM((TOP_K, tc, d), F32), pltpu.SemaphoreType.DMA],
        compiler_params=_cparams(1),
        name="moe_combine",
    )(pos2, gates, h1, g, b, ys)


def _moe_layer(alpha, h, mix, ln1_g, ln1_b, router_w, router_b, w_gu, b_gu, w_down, b_down, ln2_g, ln2_b, layer=0):
    n, d = h.shape
    n_exp = router_w.shape[1]
    ff = w_down.shape[1]
    first_expert = layer * n_exp
    rwp = jnp.pad(router_w, ((0, 0), (0, LANES - n_exp)))
    rbp = jnp.pad(router_b.reshape(1, n_exp), ((0, 0), (0, LANES - n_exp)), constant_values=NEG_BIG)
    h1, idx, gates, rank, cnt = _ln_router(alpha, h, mix, ln1_g.reshape(1, d), ln1_b.reshape(1, d), rwp, rbp)
    counts = cnt[0, :n_exp].astype(jnp.int32)
    padded = ((counts + MOE_TM - 1) // MOE_TM) * MOE_TM
    offsets = jnp.cumsum(padded) - padded
    pos = offsets[idx[:, :TOP_K]] + rank[:, :TOP_K]
    max_tiles = (n * TOP_K) // MOE_TM + n_exp
    p_rows = max_tiles * MOE_TM
    xs = _dispatch(h1, pos, p_rows).astype(BF16)
    tn1 = _pick_tile(ff, (512, 256, 128))
    tn2 = _pick_tile(d, (1024, 512, 256, 128))
    act = _gemm1(_moe_schedule(counts, ff // tn1, max_tiles, first_expert), xs, w_gu, b_gu, tn1)
    ys = _gemm2(_moe_schedule(counts, d // tn2, max_tiles, first_expert), act, w_down, b_down, tn2)
    return _combine(alpha, pos, gates, h1, ln2_g.reshape(1, d), ln2_b.reshape(1, d), ys)


def _pad_lanes(x, width=LANES, value=0.0):
    return jnp.pad(x.reshape(1, -1), ((0, 0), (0, width - x.size)), constant_values=value)


def _token_mixers(us, row0, b, t, st, p):
    u_ssd, u_ml, u_hg, u_rw = us
    ssd_conv, ssd_s, ml_c, ml_n, ml_m, hg_s, rw_shift, rw_s = st
    ya, conv_new, ssd_new = _ssd_mixer(u_ssd, row0, b, t, ssd_conv, ssd_s, p["conv_w"], p["conv_b"], p["dt_bias"],
                                       p["a_log"], p["d_skip"], p["ssd_nw"])
    yb, c_new, n_new, m_new = _mlstm_mixer(u_ml, row0, b, t, ml_c, ml_n,
                                           jnp.pad(ml_m, ((0, 0), (0, LANES - ML_HEADS))).reshape(b, 1, LANES),
                                           p["ml_bif"], p["ml_nw"])
    yc, hg_new = _gla_mixer(u_hg, row0, b, t, jnp.swapaxes(hg_s, -1, -2), p["hg_lb"], p["hg_nw"])
    prep = _rw_prep(u_rw, row0, b, t, rw_shift.reshape(b, 1, RW_IN), p["rw_mu"], p["rw_w0"], p["rw_w2p"], p["rw_a0"],
                    p["rw_a2p"], p["rw_g2"], p["rw_kk"], p["rw_ka"])
    yd, rw_new = _rw_scan(prep[:7], b, t, jnp.transpose(rw_s, (0, 2, 1, 3)), p["rw_rk"], p["rw_lnw"], p["rw_lnb"])
    y = jnp.concatenate([ya, yb, yc, yd.reshape(b * t, GROUP_W).astype(BF16)], axis=-1)
    new = (conv_new, ssd_new, c_new, n_new, m_new[:, 0, :ML_HEADS], jnp.swapaxes(hg_new, -1, -2),
           prep[7].reshape(b, RW_IN), jnp.transpose(rw_new, (0, 2, 1, 3)))
    return y, new


def _split_w_in(w):
    o1 = SSD_IN
    o2 = o1 + ML_IN
    o3 = o2 + HG_IN
    w_ssd = jnp.pad(w[:, :o1], ((0, 0), (0, SSD_INP - SSD_IN)))
    ml = w[:, o1:o2]
    w_ml = jnp.concatenate([ml[:, :ML_GATE_OFF], jnp.pad(ml[:, ML_GATE_OFF:ML_GATE_OFF + 2 * ML_HEADS],
                                                        ((0, 0), (0, LANES - 2 * ML_HEADS))),
                            ml[:, ML_GATE_OFF + 2 * ML_HEADS:]], axis=1)
    w_ml = jnp.pad(w_ml, ((0, 0), (0, ML_INP - w_ml.shape[1])))
    return tuple(x.astype(BF16) for x in (w_ssd, w_ml, w[:, o2:o3], w[:, o3:]))


def kernel(x_prompt, x_sample, state_ssd_conv, state_ssd, state_mlstm_c, state_mlstm_n, state_mlstm_m, state_hgrn, state_rwkv_shift, state_rwkv, w_in, ssd_conv_w, ssd_conv_b, ssd_dt_bias, ssd_a_log, ssd_d, ssd_norm_w, ml_b_if, ml_norm_w, hg_lower_bounds, hg_norm_w, rw_mu, rw_w0, rw_w2, rw_a0, rw_a2, rw_g2, rw_k_k, rw_k_a, rw_r_k, rw_ln_w, rw_ln_b, w_out, ln1_g, ln1_b, router_w, router_b, exp_w_gu, exp_b_gu, exp_w_down, exp_b_down, ln2_g, ln2_b):
    bp, tp, d = x_prompt.shape
    bs, ts, _ = x_sample.shape
    n_p = bp * tp
    depth = w_in.shape[0]
    alpha = float((2 * depth) ** 0.25)
    caches = (state_ssd_conv, state_ssd, state_mlstm_c, state_mlstm_n, state_mlstm_m,
              state_hgrn, state_rwkv_shift, state_rwkv)
    lbs = jax.nn.softmax(hg_lower_bounds.astype(F32), axis=0)
    lbs = jnp.cumsum(lbs, axis=0) - lbs[0]
    h = jnp.concatenate([x_prompt.reshape(n_p, d), x_sample.reshape(bs * ts, d)], axis=0)
    hb = h.astype(BF16)
    w_gu_all = exp_w_gu.reshape((-1,) + exp_w_gu.shape[2:])
    b_gu_all = exp_b_gu.reshape((-1,) + exp_b_gu.shape[2:])
    w_down_all = exp_w_down.reshape((-1,) + exp_w_down.shape[2:])
    b_down_all = exp_b_down.reshape((-1,) + exp_b_down.shape[2:])
    new_p, new_s = [], []
    for l in range(depth):
        p = dict(
            conv_w=ssd_conv_w[l], conv_b=ssd_conv_b[l].reshape(1, -1), dt_bias=_pad_lanes(ssd_dt_bias[l]),
            a_log=_pad_lanes(ssd_a_log[l]), d_skip=_pad_lanes(ssd_d[l]), ssd_nw=ssd_norm_w[l].reshape(1, -1),
            ml_bif=_pad_lanes(ml_b_if[l]), ml_nw=ml_norm_w[l].reshape(1, -1),
            hg_lb=lbs[l].reshape(1, -1), hg_nw=hg_norm_w[l].reshape(1, -1),
            rw_mu=rw_mu[l].reshape(1, -1), rw_w0=rw_w0[l].reshape(1, -1),
            rw_w2p=jnp.pad(rw_w2[l], ((0, RW_LR_A), (0, 0))), rw_a0=rw_a0[l].reshape(1, -1),
            rw_a2p=jnp.pad(rw_a2[l], ((RW_LR_W, 0), (0, 0))), rw_g2=rw_g2[l],
            rw_kk=rw_k_k[l].reshape(1, -1), rw_ka=rw_k_a[l].reshape(1, -1), rw_rk=rw_r_k[l],
            rw_lnw=rw_ln_w[l].reshape(RW_HEADS, RW_HEADDIM), rw_lnb=rw_ln_b[l].reshape(RW_HEADS, RW_HEADDIM),
        )
        us = tuple(_matmul(hb, w) for w in _split_w_in(w_in[l]))
        st_p = tuple(jnp.zeros((bp,) + c.shape[2:], F32) for c in caches)
        st_s = tuple(c[l] for c in caches)
        yp, sp = _token_mixers(us, 0, bp, tp, st_p, p)
        ys, ss = _token_mixers(us, n_p, bs, ts, st_s, p)
        mix = _matmul(jnp.concatenate([yp, ys], axis=0), w_out[l].astype(BF16))
        h, hb = _moe_layer(alpha, h, mix, ln1_g[l], ln1_b[l], router_w[l], router_b[l], w_gu_all, b_gu_all,
                           w_down_all, b_down_all, ln2_g[l], ln2_b[l], layer=l)
        new_p.append(sp)
        new_s.append(ss)
    outs_p = [jnp.stack([s[i] for s in new_p]) for i in range(8)]
    outs_s = [jnp.stack([s[i] for s in new_s]) for i in range(8)]
    y_prompt = h[:n_p].reshape(bp, tp, d)
    y_sample = h[n_p:].reshape(bs, ts, d)
    res = [y_prompt, y_sample]
    for a, b_ in zip(outs_p, outs_s):
        res += [a, b_]
    return tuple(res)
```
